```python
import math
import jax, jax.numpy as jnp
from jax import lax
import numpy as np

D_MODEL = 1024
BATCH = 32
SEQ = 256
DEPTH = 4
DEC_BATCH = 4
DEC_SEQ = 2048
PAST_LEN = 256

GRID_W = 64
EPS = 1e-6
N_MOD = 6

POOL_GROUPS = 4
POOL_WINDOWS = (2, 4, 8, 16)
POOL_DIM = D_MODEL // 2
POOL_GROUP_DIM = POOL_DIM // POOL_GROUPS

ATT_HEADS = 4
ATT_QK_DIM = 64
ATT_V_DIM = 2 * ATT_QK_DIM
ATT_QK_TOTAL = ATT_HEADS * 2 * ATT_QK_DIM
ATT_DIM = ATT_HEADS * ATT_V_DIM
ROPE_BASE = 10000.0
Q_BLOCK = 128

SSD_DIM = D_MODEL // 2
SSD_HEAD_DIM = 64
SSD_HEADS = SSD_DIM // SSD_HEAD_DIM
SSD_GROUPS = 2
SSD_STATE = 128
SSD_CONV_W = 3
SSD_CHUNK = 128
SSD_XBC_DIM = SSD_DIM + 2 * SSD_GROUPS * SSD_STATE

MIX_DIM = POOL_DIM + ATT_DIM + SSD_DIM
IN_SIZES = (POOL_DIM, ATT_QK_TOTAL, ATT_QK_TOTAL, ATT_DIM, SSD_DIM, SSD_XBC_DIM, 2 * SSD_HEADS)
IN_DIM = sum(IN_SIZES)

FFN_DIM = ((8 * D_MODEL // 3 + 127) // 128) * 128
FFN_CONV_W = 3

kernel_name = 'hybrid_flow_prefix_trunk'


def split_sizes(x, sizes, axis=-1):
    idx, acc = [], 0
    for s in sizes[:-1]:
        acc += s
        idx.append(acc)
    return jnp.split(x, idx, axis=axis)


def rmsnorm(x, g):
    xf = x.astype(jnp.float32)
    y = xf * lax.rsqrt(jnp.mean(xf * xf, axis=-1, keepdims=True) + EPS)
    return (y * g.astype(jnp.float32)).astype(x.dtype)


def dwconv_centred(x, w, b):
    k = w.shape[0]
    y = lax.conv_general_dilated(x, w[:, None, :].astype(x.dtype), window_strides=(1,),
                                 padding=[(k // 2, k // 2)],
                                 dimension_numbers=('NWC', 'WIO', 'NWC'),
                                 feature_group_count=x.shape[-1])
    return y + b.astype(x.dtype)


def multiscale_pool_mixer(u, pool_w, pool_scale):
    b, n, _ = u.shape
    uf = u.astype(jnp.float32).reshape(b, n, POOL_GROUPS, POOL_GROUP_DIM)
    cs = jnp.concatenate([jnp.zeros_like(uf[:, :1]), jnp.cumsum(uf, axis=1)], axis=1)
    half = jnp.array([w // 2 for w in POOL_WINDOWS], jnp.int32)
    t = jnp.arange(n, dtype=jnp.int32)[:, None]
    lo = jnp.clip(t - half, 0, n - 1)
    hi = jnp.clip(t + half - 1, 0, n - 1)
    g = jnp.arange(POOL_GROUPS, dtype=jnp.int32)[None, :]
    win_sum = cs[:, hi + 1, g, :] - cs[:, lo, g, :]
    count = (hi - lo + 1).astype(jnp.float32)[None, :, :, None]
    pooled = (win_sum / count - uf).astype(u.dtype)
    mixed = jnp.einsum('blgd,gde->blge', pooled, pool_w) * pool_scale.reshape(POOL_GROUPS, POOL_GROUP_DIM)
    return mixed.reshape(b, n, POOL_DIM)


def axial_rope_angles(n_tokens):
    rows = n_tokens // GRID_W
    row = jnp.repeat(jnp.arange(rows), GRID_W).astype(jnp.float32)
    col = jnp.tile(jnp.arange(GRID_W), rows).astype(jnp.float32)
    half = ATT_QK_DIM // 2
    inv = ROPE_BASE ** (-jnp.arange(0, half, 2, dtype=jnp.float32) / half)
    return row[:, None] * inv, col[:, None] * inv


def rope_rotate(x, ang):
    x1, x2 = jnp.split(x, 2, axis=-1)
    cos = jnp.cos(ang).astype(x.dtype)
    sin = jnp.sin(ang).astype(x.dtype)
    return jnp.concatenate([x1 * cos - x2 * sin, x2 * cos + x1 * sin], axis=-1)


def apply_axial_rope(x, ang_row, ang_col):
    half = ATT_QK_DIM // 2
    ar = ang_row[None, :, None, None, :]
    ac = ang_col[None, :, None, None, :]
    return jnp.concatenate([rope_rotate(x[..., :half], ar), rope_rotate(x[..., half:], ac)], axis=-1)


def diff_attention(q, k, v, lam):
    b, nq = q.shape[:2]
    nblk = nq // Q_BLOCK
    qb = jnp.moveaxis(q.reshape(b, nblk, Q_BLOCK, ATT_HEADS, 2, ATT_QK_DIM), 1, 0)
    scale = ATT_QK_DIM ** -0.5

    def block(qi):
        s = jnp.einsum('bqhmd,bkhmd->bhmqk', qi, k, preferred_element_type=jnp.float32) * scale
        p = jax.nn.softmax(s, axis=-1)
        a = p[:, :, 0] - lam * p[:, :, 1]
        return jnp.einsum('bhqk,bkhd->bqhd', a.astype(v.dtype), v)

    o = lax.map(block, qb)
    return jnp.moveaxis(o, 0, 1).reshape(b, nq, ATT_HEADS, ATT_V_DIM)


def segsum(a):
    q = a.shape[-1]
    x = jnp.broadcast_to(a[..., None], a.shape + (q,))
    x = jnp.where(jnp.tril(jnp.ones((q, q), bool), -1), x, 0.0)
    s = jnp.cumsum(x, axis=-2)
    return jnp.where(jnp.tril(jnp.ones((q, q), bool)), s, -jnp.inf)


def ssd_chunked_scan(x, dt, a, bm, cm, h0):
    b, n, h, pdim = x.shape
    nc, q = n // SSD_CHUNK, SSD_CHUNK
    xdt = (x.astype(jnp.float32) * dt[..., None]).reshape(b, nc, q, h, pdim)
    bc = bm.astype(jnp.float32).reshape(b, nc, q, h, SSD_STATE)
    cc = cm.astype(jnp.float32).reshape(b, nc, q, h, SSD_STATE)
    adt = (dt * a).reshape(b, nc, q, h).transpose(0, 3, 1, 2)
    a_cs = jnp.cumsum(adt, axis=-1)
    decay_in = jnp.exp(segsum(adt))
    y_diag = jnp.einsum('bclhn,bcshn,bhcls,bcshp->bclhp', cc, bc, decay_in, xdt)
    decay_to_end = jnp.exp(a_cs[..., -1:] - a_cs)
    chunk_states = jnp.einsum('bclhn,bhcl,bclhp->bchpn', bc, decay_to_end, xdt)
    chunk_decay = jnp.exp(a_cs[..., -1])

    def step(state, inp):
        st, dec = inp
        return state * dec[..., None, None] + st, state

    h_final, h_enter = lax.scan(step, h0.astype(jnp.float32),
                                (jnp.moveaxis(chunk_states, 1, 0), jnp.moveaxis(chunk_decay, 2, 0)))
    h_enter = jnp.moveaxis(h_enter, 0, 1)
    y_off = jnp.einsum('bclhn,bchpn,bhcl->bclhp', cc, h_enter, jnp.exp(a_cs))
    return (y_diag + y_off).reshape(b, n, h, pdim), h_final


def ssd_mixer(z, xbc, dt_raw, p, h0_f, h0_b):
    b, n, _ = xbc.shape
    xbc = jax.nn.silu(dwconv_centred(xbc, p['ssd_conv_w'], p['ssd_conv_b']))
    xs, bs, cs = split_sizes(xbc, (SSD_DIM, SSD_GROUPS * SSD_STATE, SSD_GROUPS * SSD_STATE))
    rep = SSD_HEADS // SSD_GROUPS
    x = xs.reshape(b, n, SSD_HEADS, SSD_HEAD_DIM)
    bm = jnp.repeat(bs.reshape(b, n, SSD_GROUPS, SSD_STATE), rep, axis=2)
    cm = jnp.repeat(cs.reshape(b, n, SSD_GROUPS, SSD_STATE), rep, axis=2)
    dt = jax.nn.softplus(dt_raw.astype(jnp.float32).reshape(b, n, 2, SSD_HEADS)
                         + p['ssd_dt_bias'].astype(jnp.float32))
    a = -jnp.exp(p['ssd_A_log'].astype(jnp.float32))
    y_f, h_f = ssd_chunked_scan(x, dt[:, :, 0], a[0], bm, cm, h0_f)
    flip = lambda t: jnp.flip(t, axis=1)
    y_b, h_b = ssd_chunked_scan(flip(x), flip(dt[:, :, 1]), a[1], flip(bm), flip(cm), h0_b)
    y = y_f + flip(y_b) + p['ssd_D'].astype(jnp.float32)[:, None] * x.astype(jnp.float32)
    yz = y.reshape(b, n, SSD_DIM) * jax.nn.silu(z.astype(jnp.float32))
    yz = yz.reshape(b, n, SSD_GROUPS, SSD_DIM // SSD_GROUPS)
    yz = yz * lax.rsqrt(jnp.mean(yz * yz, axis=-1, keepdims=True) + EPS)
    out = yz.reshape(b, n, SSD_DIM) * p['ssd_norm_g'].astype(jnp.float32)
    return out.astype(z.dtype), h_f, h_b


def trunk_layer(x, mod, p, lidx, rope_ang, ctx_kv, h0):
    b, n, _ = x.shape
    sh1, sc1, g1, sh2, sc2, g2 = jnp.split(mod[:, None, :].astype(x.dtype), N_MOD, axis=-1)
    h = rmsnorm(x, p['norm1_g']) * (1 + sc1) + sh1
    u, q, k, v, z, xbc, dt_raw = split_sizes(h @ p['w_in'], IN_SIZES)

    o_pool = multiscale_pool_mixer(u, p['pool_w'], p['pool_scale'])

    q = q.reshape(b, n, ATT_HEADS, 2, ATT_QK_DIM)
    k = k.reshape(b, n, ATT_HEADS, 2, ATT_QK_DIM)
    v = v.reshape(b, n, ATT_HEADS, ATT_V_DIM)
    if ctx_kv is None:
        k_keys, v_keys = k, v
    else:
        ang_r, ang_c = rope_ang
        q = apply_axial_rope(q, ang_r, ang_c)
        k_keys = jnp.concatenate([ctx_kv[0].astype(k.dtype), apply_axial_rope(k, ang_r, ang_c)], axis=1)
        v_keys = jnp.concatenate([ctx_kv[1].astype(v.dtype), v], axis=1)
    lam_init = 0.8 - 0.6 * math.exp(-0.3 * lidx)
    lp = p['att_lambda'].astype(jnp.float32)
    lam = jnp.exp(jnp.sum(lp[0] * lp[1])) - jnp.exp(jnp.sum(lp[2] * lp[3])) + lam_init
    o_att = diff_attention(q, k_keys, v_keys, lam)
    o_att = (rmsnorm(o_att, p['att_subln_g']) * (1 - lam_init)).reshape(b, n, ATT_DIM)

    o_ssd, h_f, h_b = ssd_mixer(z, xbc, dt_raw, p, h0[0], h0[1])

    gate_pool, gate_att, gate_ssd = jnp.split(jax.nn.sigmoid(h @ p['w_gate'] + p['b_gate']), 3, axis=-1)
    wb_pool, wb_att, wb_ssd = split_sizes(p['w_branch'], (POOL_DIM, ATT_DIM, SSD_DIM), axis=0)
    merged = gate_pool * (o_pool @ wb_pool) + gate_att * (o_att @ wb_att) + gate_ssd * (o_ssd @ wb_ssd)
    x = x + g1 * (merged @ p['w_out'])

    h2 = rmsnorm(x, p['norm2_g']) * (1 + sc2) + sh2
    up = dwconv_centred(h2 @ p['ffn_w_up'], p['ffn_conv_w'], p['ffn_conv_b'])
    a_gate, a_val = jnp.split(up, 2, axis=-1)
    x = x + g2 * ((jax.nn.silu(a_gate) * a_val) @ p['ffn_w_down'])
    return x, k, v, h_f, h_b


def setup_inputs(seed: int = 0) -> dict:
    key = jax.random.key(seed)
    keys = list(jax.random.split(key, 40))
    f32 = jnp.float32

    def nrm(shape, scale):
        return scale * jax.random.normal(keys.pop(), shape, f32)

    dt_init = jnp.exp(jax.random.uniform(keys.pop(), (DEPTH, 2, SSD_HEADS), f32,
                                         math.log(1e-3), math.log(1e-1)))
    a_init = jax.random.uniform(keys.pop(), (DEPTH, 2, SSD_HEADS), f32, 1.0, 16.0)
    return {
        'x_prompt': nrm((BATCH, SEQ, D_MODEL), 1.0),
        'x_sample': nrm((DEC_BATCH, DEC_SEQ, D_MODEL), 1.0),
        'c': nrm((DEC_BATCH, D_MODEL), 1.0),
        'cache_attn_k': nrm((DEC_BATCH, DEPTH, PAST_LEN, ATT_HEADS, 2, ATT_QK_DIM), 1.0),
        'cache_attn_v': nrm((DEC_BATCH, DEPTH, PAST_LEN, ATT_HEADS, ATT_V_DIM), 1.0),
        'state_ssd': nrm((DEC_BATCH, DEPTH, 2, SSD_HEADS, SSD_HEAD_DIM, SSD_STATE), 0.1),
        'c_ctx': nrm((D_MODEL,), 1.0),
        'w_mod': nrm((DEPTH, D_MODEL, N_MOD * D_MODEL), 0.5 * D_MODEL ** -0.5),
        'b_mod': nrm((DEPTH, N_MOD * D_MODEL), 0.02),
        'norm1_g': 1.0 + nrm((DEPTH, D_MODEL), 0.05),
        'w_in': nrm((DEPTH, D_MODEL, IN_DIM), D_MODEL ** -0.5),
        'pool_w': nrm((DEPTH, POOL_GROUPS, POOL_GROUP_DIM, POOL_GROUP_DIM), POOL_GROUP_DIM ** -0.5),
        'pool_scale': 1.0 + nrm((DEPTH, POOL_DIM), 0.1),
        'att_lambda': nrm((DEPTH, 4, ATT_QK_DIM), 0.1),
        'att_subln_g': 1.0 + nrm((DEPTH, ATT_V_DIM), 0.05),
        'ssd_conv_w': nrm((DEPTH, SSD_CONV_W, SSD_XBC_DIM), SSD_CONV_W ** -0.5),
        'ssd_conv_b': nrm((DEPTH, SSD_XBC_DIM), 0.02),
        'ssd_dt_bias': dt_init + jnp.log(-jnp.expm1(-dt_init)),
        'ssd_A_log': jnp.log(a_init),
        'ssd_D': 1.0 + nrm((DEPTH, SSD_HEADS), 0.1),
        'ssd_norm_g': 1.0 + nrm((DEPTH, SSD_DIM), 0.05),
        'w_gate': nrm((DEPTH, D_MODEL, 3 * D_MODEL), D_MODEL ** -0.5),
        'b_gate': nrm((DEPTH, 3 * D_MODEL), 0.02),
        'w_branch': nrm((DEPTH, MIX_DIM, D_MODEL), POOL_DIM ** -0.5),
        'w_out': nrm((DEPTH, D_MODEL, D_MODEL), D_MODEL ** -0.5),
        'norm2_g': 1.0 + nrm((DEPTH, D_MODEL), 0.05),
        'ffn_w_up': nrm((DEPTH, D_MODEL, 2 * FFN_DIM), D_MODEL ** -0.5),
        'ffn_conv_w': nrm((DEPTH, FFN_CONV_W, 2 * FFN_DIM), FFN_CONV_W ** -0.5),
        'ffn_conv_b': nrm((DEPTH, 2 * FFN_DIM), 0.02),
        'ffn_w_down': nrm((DEPTH, FFN_DIM, D_MODEL), FFN_DIM ** -0.5),
        'final_norm_g': 1.0 + nrm((D_MODEL,), 0.05),
    }


def reference(x_prompt, x_sample, c, cache_attn_k, cache_attn_v, state_ssd, c_ctx, w_mod, b_mod,
              norm1_g, w_in, pool_w, pool_scale, att_lambda, att_subln_g, ssd_conv_w, ssd_conv_b,
              ssd_dt_bias, ssd_A_log, ssd_D, ssd_norm_g, w_gate, b_gate, w_branch, w_out, norm2_g,
              ffn_w_up, ffn_conv_w, ffn_conv_b, ffn_w_down, final_norm_g):
    params = [dict(norm1_g=norm1_g[l], w_in=w_in[l], pool_w=pool_w[l], pool_scale=pool_scale[l],
                   att_lambda=att_lambda[l], att_subln_g=att_subln_g[l], ssd_conv_w=ssd_conv_w[l],
                   ssd_conv_b=ssd_conv_b[l], ssd_dt_bias=ssd_dt_bias[l], ssd_A_log=ssd_A_log[l],
                   ssd_D=ssd_D[l], ssd_norm_g=ssd_norm_g[l], w_gate=w_gate[l], b_gate=b_gate[l],
                   w_branch=w_branch[l], w_out=w_out[l], norm2_g=norm2_g[l], ffn_w_up=ffn_w_up[l],
                   ffn_conv_w=ffn_conv_w[l], ffn_conv_b=ffn_conv_b[l], ffn_w_down=ffn_w_down[l])
              for l in range(DEPTH)]

    xp = x_prompt
    h0_zero = jnp.zeros((x_prompt.shape[0], SSD_HEADS, SSD_HEAD_DIM, SSD_STATE), jnp.float32)
    new_k, new_v, new_s = [], [], []
    for l in range(DEPTH):
        mod_ctx = jax.nn.silu(c_ctx)[None, :] @ w_mod[l] + b_mod[l]
        xp, k_l, v_l, hf_l, hb_l = trunk_layer(xp, mod_ctx, params[l], l, None, None, (h0_zero, h0_zero))
        new_k.append(k_l)
        new_v.append(v_l)
        new_s.append(jnp.stack([hf_l, hb_l], axis=1))

    xs = x_sample
    rope_ang = axial_rope_angles(x_sample.shape[1])
    for l in range(DEPTH):
        mod_lat = jax.nn.silu(c) @ w_mod[l] + b_mod[l]
        xs = trunk_layer(xs, mod_lat, params[l], l, rope_ang,
                         (cache_attn_k[:, l], cache_attn_v[:, l]),
                         (state_ssd[:, l, 0], state_ssd[:, l, 1]))[0]

    y_prompt = rmsnorm(xp, final_norm_g)
    y_sample = rmsnorm(xs, final_norm_g)
    new_cache_attn_k = jnp.stack(new_k, axis=1)
    new_cache_attn_v = jnp.stack(new_v, axis=1)
    new_state_ssd = jnp.stack(new_s, axis=1)
    return (y_prompt, y_sample, new_cache_attn_k, new_cache_attn_v, new_state_ssd)
```

```python
import functools
import math

import numpy as np
import jax
import jax.numpy as jnp
from jax import lax
from jax.experimental import pallas as pl
from jax.experimental.pallas import tpu as pltpu

D_MODEL = 1024
BATCH = 32
SEQ = 256
DEPTH = 4
DEC_BATCH = 4
DEC_SEQ = 2048
PAST_LEN = 256
GRID_W = 64
EPS = 1e-6
N_MOD = 6

POOL_GROUPS = 4
POOL_WINDOWS = (2, 4, 8, 16)
POOL_DIM = D_MODEL // 2
POOL_GROUP_DIM = POOL_DIM // POOL_GROUPS

ATT_HEADS = 4
ATT_QK_DIM = 64
ATT_V_DIM = 2 * ATT_QK_DIM
ATT_QK_TOTAL = ATT_HEADS * 2 * ATT_QK_DIM
ATT_DIM = ATT_HEADS * ATT_V_DIM
ROPE_BASE = 10000.0

SSD_DIM = D_MODEL // 2
SSD_HEAD_DIM = 64
SSD_HEADS = SSD_DIM // SSD_HEAD_DIM
SSD_GROUPS = 2
SSD_STATE = 128
SSD_CHUNK = 128
SSD_XBC_DIM = SSD_DIM + 2 * SSD_GROUPS * SSD_STATE
SSD_GROUP_DIM = SSD_DIM // SSD_GROUPS

MIX_DIM = POOL_DIM + ATT_DIM + SSD_DIM
IN_SIZES = (POOL_DIM, ATT_QK_TOTAL, ATT_QK_TOTAL, ATT_DIM, SSD_DIM, SSD_XBC_DIM, 2 * SSD_HEADS)
FFN_DIM = ((8 * D_MODEL // 3 + 127) // 128) * 128

LANE = 128
SUBLANE = 8
HALO = SUBLANE

TILE = 256
N_CTX = BATCH * SEQ
N_LAT = DEC_BATCH * DEC_SEQ
N_TOK = N_CTX + N_LAT
NT_CTX = N_CTX // TILE
NT = N_TOK // TILE
TILES_PER_LAT = DEC_SEQ // TILE
N_HALO_BLOCKS = N_TOK // HALO
MOD_ROWS = 8
DT_PAD = LANE
FFN_CHUNK = FFN_DIM // 2
VMEM_LIMIT = 56 * 1024 * 1024

F32 = jnp.float32
BF16 = jnp.bfloat16

assert SEQ == TILE and DEC_SEQ % TILE == 0 and PAST_LEN == TILE
assert TILE == 2 * SSD_CHUNK and SSD_STATE == LANE and FFN_CHUNK % LANE == 0


def _dot(a, b):
    return jnp.dot(a, b, preferred_element_type=F32)


def _dot_nt(a, b):
    return lax.dot_general(a, b, (((1,), (1,)), ((), ())), preferred_element_type=F32)


def _split3(x):
    x1 = x.astype(BF16)
    r1 = x - x1.astype(F32)
    x2 = r1.astype(BF16)
    x3 = (r1 - x2.astype(F32)).astype(BF16)
    return x1, x2, x3


def _dot_exact_rhs(a, b_bf16):
    a1, a2, a3 = _split3(a)
    return _dot(a1, b_bf16) + _dot(a2, b_bf16) + _dot(a3, b_bf16)


def _dot_exact_lhs(a_bf16, b):
    b1, b2, b3 = _split3(b)
    return _dot(a_bf16, b1) + _dot(a_bf16, b2) + _dot(a_bf16, b3)


def _silu(x):
    return x * jax.nn.sigmoid(x)


def _tile_pos(t):
    is_ctx = t < NT_CTX
    j = jnp.maximum(t - NT_CTX, 0) % TILES_PER_LAT
    return is_ctx, j


def _lat_batch(t):
    return jnp.maximum(t - NT_CTX, 0) // TILES_PER_LAT


def _mod_row(t):
    return jnp.where(t < NT_CTX, 0, 1 + _lat_batch(t))


def _prev_halo(t):
    return jnp.maximum(t * (TILE // HALO) - 1, 0)


def _next_halo(t):
    return jnp.minimum((t + 1) * (TILE // HALO), N_HALO_BLOCKS - 1)


def _const_spec(shape):
    nd = len(shape)
    return pl.BlockSpec(shape, lambda *_: (0,) * nd, pipeline_mode=pl.Buffered(1))


def _row_spec(cols, fn=lambda i: i):
    return pl.BlockSpec((TILE, cols), lambda i: (fn(i), 0))


def _halo_spec(cols, which, fn=lambda i: i):
    pick = _prev_halo if which == "prev" else _next_halo
    return pl.BlockSpec((HALO, cols), lambda i: (pick(fn(i)), 0))


def _params(n_axes=1):
    return pltpu.CompilerParams(dimension_semantics=("arbitrary",) * n_axes,
                                vmem_limit_bytes=VMEM_LIMIT)


def _rmsnorm_rows(x, g):
    ms = jnp.mean(x * x, axis=-1, keepdims=True)
    return x * lax.rsqrt(ms + EPS) * g


MOD_COLS = 1536


def _mod_kernel(c_ref, w_ref, b_ref, o_ref):
    s = _silu(c_ref[...]).astype(BF16)
    o_ref[0] = _dot(s, w_ref[0].astype(BF16)) + b_ref[0]


def _modulation(cvec, w_mod, b_mod):
    nmod = N_MOD * D_MODEL
    return pl.pallas_call(
        _mod_kernel,
        out_shape=jax.ShapeDtypeStruct((DEPTH, MOD_ROWS, nmod), F32),
        grid=(DEPTH, nmod // MOD_COLS),
        in_specs=[pl.BlockSpec((MOD_ROWS, D_MODEL), lambda l, j: (0, 0)),
                  pl.BlockSpec((1, D_MODEL, MOD_COLS), lambda l, j: (l, 0, j)),
                  pl.BlockSpec((1, 1, MOD_COLS), lambda l, j: (l, 0, j))],
        out_specs=pl.BlockSpec((1, MOD_ROWS, MOD_COLS), lambda l, j: (l, 0, j)),
        compiler_params=_params(2),
        name="modulation",
    )(cvec, w_mod, b_mod.reshape(DEPTH, 1, nmod))


_C_POOL = 0
_C_Q = _C_POOL + POOL_DIM
_C_K = _C_Q + ATT_QK_TOTAL
_C_V = _C_K + ATT_QK_TOTAL
_C_Z = _C_V + ATT_DIM
_C_XBC = _C_Z + SSD_DIM
_C_DT = _C_XBC + SSD_XBC_DIM
IN_COLS = _C_DT + DT_PAD
ROPE_SWAP = ATT_QK_DIM // 4


def _rope_block(xb, cos, sin):
    lane = lax.broadcasted_iota(jnp.int32, xb.shape, 1)
    first_half = (lane % (2 * ROPE_SWAP)) < ROPE_SWAP
    partner = jnp.where(first_half,
                        pltpu.roll(xb, LANE - ROPE_SWAP, axis=1),
                        pltpu.roll(xb, ROPE_SWAP, axis=1))
    return xb * cos + partner * sin


def _inproj_kernel(x_ref, mod_ref, g_ref, w_ref, wg_ref, bg_ref, cos_ref, sin_ref,
                   u_ref, q_ref, k_ref, v_ref, z_ref, xbc_ref, dt_ref, gate_ref):
    i = pl.program_id(0)
    h = _rmsnorm_rows(x_ref[...], g_ref[...]) * (1.0 + mod_ref[0, 1:2, :]) + mod_ref[0, 0:1, :]
    hb = h.astype(BF16)
    u_ref[...] = _dot(hb, w_ref[:, _C_POOL:_C_Q])
    v_ref[...] = _dot(hb, w_ref[:, _C_V:_C_Z])
    z_ref[...] = _dot(hb, w_ref[:, _C_Z:_C_XBC])
    xbc_ref[...] = _dot(hb, w_ref[:, _C_XBC:_C_DT])
    dt_ref[...] = _dot(hb, w_ref[:, _C_DT:IN_COLS])
    gate_ref[...] = jax.nn.sigmoid(_dot(hb, wg_ref[...]) + bg_ref[...]).astype(BF16)
    q = _dot(hb, w_ref[:, _C_Q:_C_K])
    k = _dot(hb, w_ref[:, _C_K:_C_V])

    @pl.when(i < NT_CTX)
    def _():
        q_ref[...] = q.astype(BF16)
        k_ref[...] = k

    @pl.when(i >= NT_CTX)
    def _():
        cos = cos_ref[...]
        sin = sin_ref[...]
        for blk in range(ATT_QK_TOTAL // LANE):
            sl = slice(blk * LANE, (blk + 1) * LANE)
            q_ref[:, sl] = _rope_block(q[:, sl], cos, sin).astype(BF16)
            k_ref[:, sl] = _rope_block(k[:, sl], cos, sin)


def _inproj(x, mod_l, norm_g, w_cat, w_gate, b_gate, rope_cos, rope_sin):
    outs = [(POOL_DIM, F32), (ATT_QK_TOTAL, BF16), (ATT_QK_TOTAL, F32), (ATT_DIM, F32),
            (SSD_DIM, F32), (SSD_XBC_DIM, F32), (DT_PAD, F32), (3 * D_MODEL, BF16)]
    rope_idx = lambda i: (jnp.maximum(i - NT_CTX, 0) % TILES_PER_LAT, 0)
    return pl.pallas_call(
        _inproj_kernel,
        out_shape=[jax.ShapeDtypeStruct((N_TOK, c), d) for c, d in outs],
        grid=(NT,),
        in_specs=[_row_spec(D_MODEL),
                  pl.BlockSpec((1, N_MOD, D_MODEL), lambda i: (_mod_row(i), 0, 0)),
                  _const_spec((1, D_MODEL)),
                  _const_spec((D_MODEL, IN_COLS)),
                  _const_spec((D_MODEL, 3 * D_MODEL)),
                  _const_spec((1, 3 * D_MODEL)),
                  pl.BlockSpec((TILE, LANE), rope_idx),
                  pl.BlockSpec((TILE, LANE), rope_idx)],
        out_specs=[_row_spec(c) for c, _ in outs],
        compiler_params=_params(),
        name="inproj",
    )(x, mod_l, norm_g, w_cat, w_gate, b_gate, rope_cos, rope_sin)


def _fill_halo_scratch(scr, prev_ref, cur, next_ref, first, last):
    scr[0:HALO, :] = jnp.where(first, 0.0, prev_ref[...])
    scr[HALO:HALO + TILE, :] = cur
    scr[HALO + TILE:, :] = jnp.where(last, 0.0, next_ref[...])


def _pool_kernel(u_ref, up_ref, un_ref, w_ref, s_ref, o_ref, scr):
    i = pl.program_id(0)
    is_ctx, j = _tile_pos(i)
    first = jnp.logical_or(is_ctx, j == 0)
    last = jnp.logical_or(is_ctx, j == TILES_PER_LAT - 1)
    _fill_halo_scratch(scr, up_ref, u_ref[...], un_ref, first, last)
    seq_len = jnp.where(is_ctx, SEQ, DEC_SEQ)
    pos = jnp.where(is_ctx, 0, j * TILE) + lax.broadcasted_iota(jnp.int32, (TILE, 1), 0)
    for g, win in enumerate(POOL_WINDOWS):
        half = win // 2
        sl = slice(g * POOL_GROUP_DIM, (g + 1) * POOL_GROUP_DIM)
        acc = scr[HALO - half:HALO - half + TILE, sl]
        for k in range(1 - half, half):
            acc = acc + scr[HALO + k:HALO + k + TILE, sl]
        lo = jnp.maximum(pos - half, 0)
        hi = jnp.minimum(pos + half - 1, seq_len - 1)
        count = (hi - lo + 1).astype(F32)
        pooled = acc / count - u_ref[:, sl]
        mixed = _dot(pooled.astype(BF16), w_ref[g]) * s_ref[:, sl]
        o_ref[:, sl] = mixed.astype(BF16)


def _pool_mixer(u, pool_w, pool_scale):
    return pl.pallas_call(
        _pool_kernel,
        out_shape=jax.ShapeDtypeStruct((N_TOK, POOL_DIM), BF16),
        grid=(NT,),
        in_specs=[_row_spec(POOL_DIM), _halo_spec(POOL_DIM, "prev"), _halo_spec(POOL_DIM, "next"),
                  _const_spec((POOL_GROUPS, POOL_GROUP_DIM, POOL_GROUP_DIM)),
                  _const_spec((1, POOL_DIM))],
        out_specs=_row_spec(POOL_DIM),
        scratch_shapes=[pltpu.VMEM((TILE + 2 * HALO, POOL_DIM), F32)],
        compiler_params=_params(),
        name="pool_mixer",
    )(u, u, u, pool_w, pool_scale)


def _attn_kernel(q_ref, kc_ref, vc_ref, kl_ref, vl_ref, kp_ref, vp_ref, lam_ref, g_ref, o_ref,
                 *, lam_init):
    i = pl.program_id(0)
    lp = lam_ref[0]
    lam = (jnp.exp(jnp.sum(lp[0:1] * lp[1:2], axis=-1, keepdims=True))
           - jnp.exp(jnp.sum(lp[2:3] * lp[3:4], axis=-1, keepdims=True)) + lam_init)
    scale = ATT_QK_DIM ** -0.5

    def run(segments):
        for h in range(ATT_HEADS):
            sl = slice(h * LANE, (h + 1) * LANE)
            qh = q_ref[:, sl]
            lane = lax.broadcasted_iota(jnp.int32, qh.shape, 1)
            ks = [kr[:, sl].astype(BF16) for kr, _ in segments]
            vs = [vr[:, sl].astype(BF16) for _, vr in segments]
            outs = []
            for m in range(2):
                in_map = (lane < ATT_QK_DIM) if m == 0 else (lane >= ATT_QK_DIM)
                qm = jnp.where(in_map, qh, jnp.zeros_like(qh))
                parts = [_dot_nt(qm, kk) for kk in ks]
                s = (parts[0] if len(parts) == 1 else jnp.concatenate(parts, axis=1)) * scale
                p = jnp.exp(s - jnp.max(s, axis=-1, keepdims=True))
                denom = jnp.sum(p, axis=-1, keepdims=True)
                pb = p.astype(BF16)
                off = 0
                acc = None
                for vv in vs:
                    part = _dot(pb[:, off:off + vv.shape[0]], vv)
                    acc = part if acc is None else acc + part
                    off += vv.shape[0]
                outs.append(acc / denom)
            oh = outs[0] - lam * outs[1]
            oh = _rmsnorm_rows(oh, g_ref[...]) * (1.0 - lam_init)
            o_ref[:, sl] = oh.astype(BF16)

    @pl.when(i < NT_CTX)
    def _():
        run([(kc_ref, vc_ref)])

    @pl.when(i >= NT_CTX)
    def _():
        run([(kp_ref.at[0, 0], vp_ref.at[0, 0]), (kl_ref, vl_ref)])


def _diff_attention(q, k, v, cache_k, cache_v, att_lambda, subln_g, lidx):
    lam_init = 0.8 - 0.6 * math.exp(-0.3 * lidx)
    ctx_idx = lambda i: (jnp.minimum(i, NT_CTX - 1), 0)
    lat_idx = lambda i: (N_CTX // DEC_SEQ + _lat_batch(i), 0)
    past_idx = lambda i: (_lat_batch(i), lidx, 0, 0)
    return pl.pallas_call(
        functools.partial(_attn_kernel, lam_init=lam_init),
        out_shape=jax.ShapeDtypeStruct((N_TOK, ATT_DIM), BF16),
        grid=(NT,),
        in_specs=[_row_spec(ATT_QK_TOTAL),
                  pl.BlockSpec((TILE, ATT_QK_TOTAL), ctx_idx),
                  pl.BlockSpec((TILE, ATT_DIM), ctx_idx),
                  pl.BlockSpec((DEC_SEQ, ATT_QK_TOTAL), lat_idx),
                  pl.BlockSpec((DEC_SEQ, ATT_DIM), lat_idx),
                  pl.BlockSpec((1, 1, PAST_LEN, ATT_QK_TOTAL), past_idx),
                  pl.BlockSpec((1, 1, PAST_LEN, ATT_DIM), past_idx),
                  pl.BlockSpec((1, 4, ATT_QK_DIM), lambda i: (lidx, 0, 0)),
                  _const_spec((1, ATT_V_DIM))],
        out_specs=_row_spec(ATT_DIM),
        compiler_params=_params(),
        name="diff_attention",
    )(q, k, v, k, v, cache_k, cache_v, att_lambda, subln_g)


def _ssd_scan_tile(direction, t, xbc_ref, xp_ref, xn_ref, dt_ref, h0_ref, cw_ref, cb_ref, dtb_ref,
                   alog_ref, asmall_ref, e_ref, st_ref, conv_scr, xc_scr, state_scr, y_scr):
    is_ctx, j = _tile_pos(t)
    first = jnp.logical_or(is_ctx, j == 0)
    last = jnp.logical_or(is_ctx, j == TILES_PER_LAT - 1)
    seq_begins = first if direction == 0 else last

    @pl.when(is_ctx)
    def _():
        state_scr[...] = jnp.zeros_like(state_scr)

    @pl.when(jnp.logical_and(jnp.logical_not(is_ctx), seq_begins))
    def _():
        state_scr[...] = h0_ref[0, 0, 0].T

    _fill_halo_scratch(conv_scr, xp_ref, xbc_ref[...], xn_ref, first, last)
    conv = (cw_ref[0:1, :] * conv_scr[HALO - 1:HALO - 1 + TILE, :]
            + cw_ref[1:2, :] * conv_scr[HALO:HALO + TILE, :]
            + cw_ref[2:3, :] * conv_scr[HALO + 1:HALO + 1 + TILE, :] + cb_ref[...])
    xc_scr[...] = _silu(conv)

    a_exp = -jnp.exp(alog_ref[...])
    a_small = -jnp.exp(asmall_ref[...])
    row = lax.broadcasted_iota(jnp.int32, (SSD_CHUNK, SSD_CHUNK), 0)
    col = lax.broadcasted_iota(jnp.int32, (SSD_CHUNK, SSD_CHUNK), 1)
    keep = (col <= row) if direction == 0 else (col >= row)
    tri = jnp.where(keep, 1.0, 0.0).astype(BF16)
    pair_lane = lax.broadcasted_iota(jnp.int32, (SSD_CHUNK, LANE), 1)
    edge = SSD_CHUNK - 1 if direction == 0 else 0
    heads_per_group = SSD_HEADS // SSD_GROUPS

    chunks = (0, 1) if direction == 0 else (1, 0)
    for c in chunks:
        rows = slice(c * SSD_CHUNK, (c + 1) * SSD_CHUNK)
        dt = jax.nn.softplus(dt_ref[rows, :] + dtb_ref[...])
        dt_exp = _dot_exact_rhs(dt, e_ref[...])
        acs_exp = _dot_exact_lhs(tri, dt_exp * a_exp)
        acs_s = _dot_exact_lhs(tri, dt * a_small)
        acs_st = acs_s.T
        xdt = xc_scr[rows, 0:SSD_DIM] * dt_exp
        total = acs_exp[edge:edge + 1, :]
        xw = (xdt * jnp.exp(total - acs_exp)).astype(BF16)
        decay_from_entry = jnp.exp(acs_exp)
        for g in range(SSD_GROUPS):
            gs = slice(g * SSD_GROUP_DIM, (g + 1) * SSD_GROUP_DIM)
            b_g = xc_scr[rows, SSD_DIM + g * SSD_STATE:SSD_DIM + (g + 1) * SSD_STATE]
            c_g = xc_scr[rows, SSD_DIM + (SSD_GROUPS + g) * SSD_STATE:
                         SSD_DIM + (SSD_GROUPS + g + 1) * SSD_STATE].astype(BF16)
            cb = _dot_nt(c_g, b_g.astype(BF16))
            h_in = state_scr[:, gs]
            y_g = _dot(c_g, h_in.astype(BF16)) * decay_from_entry[:, gs]
            pairs = []
            for pr in range(heads_per_group // 2):
                ps = slice(g * SSD_GROUP_DIM + pr * LANE, g * SSD_GROUP_DIM + (pr + 1) * LANE)
                x_pair = xdt[:, ps].astype(BF16)
                halves = []
                for hh in range(2):
                    hcol = direction * SSD_HEADS + g * heads_per_group + pr * 2 + hh
                    diff = acs_s[:, hcol:hcol + 1] - acs_st[hcol:hcol + 1, :]
                    decay = jnp.where(keep, jnp.exp(diff), 0.0)
                    halves.append(_dot((cb * decay).astype(BF16), x_pair))
                pairs.append(jnp.where(pair_lane < SSD_HEAD_DIM, halves[0], halves[1]))
            y_g = y_g + jnp.concatenate(pairs, axis=1)
            y_scr[rows, gs] = y_g
            state_scr[:, gs] = (h_in * jnp.exp(total[:, gs])
                                + _dot(b_g.T.astype(BF16), xw[:, gs]))
    st_ref[0] = state_scr[...].T


def _ssd_fwd_kernel(xbc_ref, xp_ref, xn_ref, dt_ref, h0_ref, cw_ref, cb_ref, dtb_ref, alog_ref,
                    asmall_ref, e_ref, y_ref, st_ref, conv_scr, xc_scr, state_scr, y_scr):
    t = pl.program_id(0)
    _ssd_scan_tile(0, t, xbc_ref, xp_ref, xn_ref, dt_ref, h0_ref, cw_ref, cb_ref, dtb_ref, alog_ref,
                   asmall_ref, e_ref, st_ref, conv_scr, xc_scr, state_scr, y_scr)
    y_ref[...] = y_scr[...]


def _ssd_bwd_kernel(xbc_ref, xp_ref, xn_ref, dt_ref, h0_ref, cw_ref, cb_ref, dtb_ref, alog_ref,
                    asmall_ref, e_ref, yf_ref, z_ref, dskip_ref, ng_ref, o_ref, st_ref,
                    conv_scr, xc_scr, state_scr, y_scr):
    t = NT - 1 - pl.program_id(0)
    _ssd_scan_tile(1, t, xbc_ref, xp_ref, xn_ref, dt_ref, h0_ref, cw_ref, cb_ref, dtb_ref, alog_ref,
                   asmall_ref, e_ref, st_ref, conv_scr, xc_scr, state_scr, y_scr)
    y = yf_ref[...] + y_scr[...] + dskip_ref[...] * xc_scr[:, 0:SSD_DIM]
    yz = y * _silu(z_ref[...])
    for g in range(SSD_GROUPS):
        gs = slice(g * SSD_GROUP_DIM, (g + 1) * SSD_GROUP_DIM)
        o_ref[:, gs] = _rmsnorm_rows(yz[:, gs], ng_ref[:, gs]).astype(BF16)


def _ssd_mixer(xbc, dt_raw, z, state_ssd, lidx, p):
    scratch = [pltpu.VMEM((TILE + 2 * HALO, SSD_XBC_DIM), F32),
               pltpu.VMEM((TILE, SSD_XBC_DIM), F32),
               pltpu.VMEM((SSD_STATE, SSD_DIM), F32),
               pltpu.VMEM((TILE, SSD_DIM), F32)]
    st_shape = jax.ShapeDtypeStruct((NT, SSD_DIM, SSD_STATE), F32)
    results = []
    for direction in range(2):
        order = (lambda i: i) if direction == 0 else (lambda i: NT - 1 - i)
        h0_idx = lambda i, order=order, direction=direction: (_lat_batch(order(i)), lidx, direction, 0, 0)
        common_specs = [_row_spec(SSD_XBC_DIM, order), _halo_spec(SSD_XBC_DIM, "prev", order),
                        _halo_spec(SSD_XBC_DIM, "next", order), _row_spec(DT_PAD, order),
                        pl.BlockSpec((1, 1, 1, SSD_DIM, SSD_STATE), h0_idx),
                        _const_spec((3, SSD_XBC_DIM)), _const_spec((1, SSD_XBC_DIM)),
                        _const_spec((1, DT_PAD)), _const_spec((1, SSD_DIM)), _const_spec((1, DT_PAD)),
                        _const_spec((DT_PAD, SSD_DIM))]
        common_args = [xbc, xbc, xbc, dt_raw, state_ssd, p["conv_w"], p["conv_b"], p["dt_bias"],
                       p["alog_exp"][direction], p["alog_small"][direction], p["expand"][direction]]
        st_spec = pl.BlockSpec((1, SSD_DIM, SSD_STATE), lambda i, order=order: (order(i), 0, 0))
        if direction == 0:
            y_f, st = pl.pallas_call(
                _ssd_fwd_kernel,
                out_shape=[jax.ShapeDtypeStruct((N_TOK, SSD_DIM), F32), st_shape],
                grid=(NT,),
                in_specs=common_specs,
                out_specs=[_row_spec(SSD_DIM, order), st_spec],
                scratch_shapes=scratch,
                compiler_params=_params(),
                name="ssd_forward",
            )(*common_args)
        else:
            out, st = pl.pallas_call(
                _ssd_bwd_kernel,
                out_shape=[jax.ShapeDtypeStruct((N_TOK, SSD_DIM), BF16), st_shape],
                grid=(NT,),
                in_specs=common_specs + [_row_spec(SSD_DIM, order), _row_spec(SSD_DIM, order),
                                         _const_spec((1, SSD_DIM)), _const_spec((1, SSD_DIM))],
                out_specs=[_row_spec(SSD_DIM, order), st_spec],
                scratch_shapes=scratch,
                compiler_params=_params(),
                name="ssd_backward",
            )(*common_args, y_f, z, p["d_skip"], p["norm_g"])
        results.append(st)
    return out, results[0], results[1]


def _merge_kernel(x_ref, mod_ref, gate_ref, op_ref, oa_ref, os_ref, wb_ref, wo_ref, o_ref):
    merged = (gate_ref[:, 0:D_MODEL] * _dot(op_ref[...], wb_ref[0:POOL_DIM, :])
              + gate_ref[:, D_MODEL:2 * D_MODEL] * _dot(oa_ref[...], wb_ref[POOL_DIM:POOL_DIM + ATT_DIM, :])
              + gate_ref[:, 2 * D_MODEL:] * _dot(os_ref[...], wb_ref[POOL_DIM + ATT_DIM:, :]))
    o_ref[...] = x_ref[...] + mod_ref[0, 2:3, :] * _dot(merged.astype(BF16), wo_ref[...])


def _merge(x, mod_l, gates, o_pool, o_att, o_ssd, w_branch, w_out):
    return pl.pallas_call(
        _merge_kernel,
        out_shape=jax.ShapeDtypeStruct((N_TOK, D_MODEL), F32),
        grid=(NT,),
        in_specs=[_row_spec(D_MODEL),
                  pl.BlockSpec((1, N_MOD, D_MODEL), lambda i: (_mod_row(i), 0, 0)),
                  _row_spec(3 * D_MODEL), _row_spec(POOL_DIM), _row_spec(ATT_DIM), _row_spec(SSD_DIM),
                  _const_spec((MIX_DIM, D_MODEL)), _const_spec((D_MODEL, D_MODEL))],
        out_specs=_row_spec(D_MODEL),
        compiler_params=_params(),
        name="merge",
    )(x, mod_l, gates, o_pool, o_att, o_ssd, w_branch, w_out)


def _ffn_kernel(x_ref, xp_ref, xn_ref, mod_ref, g_ref, wu_ref, cw_ref, cb_ref, wd_ref, o_ref,
                h_scr, up_scr):
    i = pl.program_id(0)
    is_ctx, j = _tile_pos(i)
    first = jnp.logical_or(is_ctx, j == 0)
    last = jnp.logical_or(is_ctx, j == TILES_PER_LAT - 1)
    scale = 1.0 + mod_ref[0, 4:5, :]
    shift = mod_ref[0, 3:4, :]

    def modulated(x):
        return _rmsnorm_rows(x, g_ref[...]) * scale + shift

    x = x_ref[...]
    h_scr[0:HALO, :] = jnp.where(first, 0.0, modulated(xp_ref[...]))
    h_scr[HALO:HALO + TILE, :] = modulated(x)
    h_scr[HALO + TILE:, :] = jnp.where(last, 0.0, modulated(xn_ref[...]))
    hb = h_scr[...].astype(BF16)

    def conv_half(col0):
        cols = slice(col0, col0 + FFN_CHUNK)
        up_scr[...] = _dot(hb, wu_ref[:, cols])
        return (cw_ref[0:1, cols] * up_scr[HALO - 1:HALO - 1 + TILE, :]
                + cw_ref[1:2, cols] * up_scr[HALO:HALO + TILE, :]
                + cw_ref[2:3, cols] * up_scr[HALO + 1:HALO + 1 + TILE, :] + cb_ref[:, cols])

    acc = None
    for c in range(FFN_DIM // FFN_CHUNK):
        a_gate = conv_half(c * FFN_CHUNK)
        a_val = conv_half(FFN_DIM + c * FFN_CHUNK)
        act = (_silu(a_gate) * a_val).astype(BF16)
        part = _dot(act, wd_ref[c * FFN_CHUNK:(c + 1) * FFN_CHUNK, :])
        acc = part if acc is None else acc + part
    o_ref[...] = x + mod_ref[0, 5:6, :] * acc


def _conv_ffn(x, mod_l, norm_g, w_up, conv_w, conv_b, w_down):
    return pl.pallas_call(
        _ffn_kernel,
        out_shape=jax.ShapeDtypeStruct((N_TOK, D_MODEL), F32),
        grid=(NT,),
        in_specs=[_row_spec(D_MODEL), _halo_spec(D_MODEL, "prev"), _halo_spec(D_MODEL, "next"),
                  pl.BlockSpec((1, N_MOD, D_MODEL), lambda i: (_mod_row(i), 0, 0)),
                  _const_spec((1, D_MODEL)),
                  _const_spec((D_MODEL, 2 * FFN_DIM)), _const_spec((3, 2 * FFN_DIM)),
                  _const_spec((1, 2 * FFN_DIM)), _const_spec((FFN_DIM, D_MODEL))],
        out_specs=_row_spec(D_MODEL),
        scratch_shapes=[pltpu.VMEM((TILE + 2 * HALO, D_MODEL), F32),
                        pltpu.VMEM((TILE + 2 * HALO, FFN_CHUNK), F32)],
        compiler_params=_params(),
        name="conv_ffn",
    )(x, x, x, mod_l, norm_g, w_up, conv_w, conv_b, w_down)


def _final_norm_kernel(x_ref, g_ref, o_ref):
    o_ref[...] = _rmsnorm_rows(x_ref[...], g_ref[...])


def _final_norm(x, g):
    return pl.pallas_call(
        _final_norm_kernel,
        out_shape=jax.ShapeDtypeStruct((N_TOK, D_MODEL), F32),
        grid=(NT,),
        in_specs=[_row_spec(D_MODEL), _const_spec((1, D_MODEL))],
        out_specs=_row_spec(D_MODEL),
        compiler_params=_params(),
        name="final_norm",
    )(x, g)


def _rope_tables():
    rows = DEC_SEQ // GRID_W
    row = jnp.repeat(jnp.arange(rows), GRID_W).astype(F32)
    col = jnp.tile(jnp.arange(GRID_W), rows).astype(F32)
    half = ATT_QK_DIM // 2
    inv = ROPE_BASE ** (-jnp.arange(0, half, 2, dtype=F32) / half)
    ar, ac = row[:, None] * inv, col[:, None] * inv
    cos = jnp.concatenate([jnp.cos(ar), jnp.cos(ar), jnp.cos(ac), jnp.cos(ac)], axis=-1)
    sin = jnp.concatenate([-jnp.sin(ar), jnp.sin(ar), -jnp.sin(ac), jnp.sin(ac)], axis=-1)
    reps = LANE // ATT_QK_DIM
    return jnp.tile(cos, (1, reps)), jnp.tile(sin, (1, reps))


def _expand_matrices():
    e = np.zeros((2, DT_PAD, SSD_DIM), np.float32)
    for d in range(2):
        for h in range(SSD_HEADS):
            e[d, d * SSD_HEADS + h, h * SSD_HEAD_DIM:(h + 1) * SSD_HEAD_DIM] = 1.0
    return jnp.asarray(e, BF16)


def _layer_weights(l, w_in, w_gate, b_gate, pool_w, pool_scale, subln_g, ssd_conv_w, ssd_conv_b,
                   ssd_dt_bias, ssd_A_log, ssd_D, ssd_norm_g, w_branch, w_out, ffn_w_up, ffn_conv_w,
                   ffn_conv_b, ffn_w_down, norm1_g, norm2_g):
    n_dt = 2 * SSD_HEADS
    w = w_in[l]
    w_cat = jnp.concatenate([w[:, :IN_COLS - DT_PAD], w[:, IN_COLS - DT_PAD:],
                             jnp.zeros((D_MODEL, DT_PAD - n_dt), F32)], axis=1).astype(BF16)
    alog = ssd_A_log[l]
    alog_small = jnp.zeros((2, 1, DT_PAD), F32)
    for d in range(2):
        alog_small = alog_small.at[d, 0, d * SSD_HEADS:(d + 1) * SSD_HEADS].set(alog[d])
    return dict(
        norm1_g=norm1_g[l][None], norm2_g=norm2_g[l][None],
        w_cat=w_cat, w_gate=w_gate[l].astype(BF16), b_gate=b_gate[l][None],
        pool_w=pool_w[l].astype(BF16), pool_scale=pool_scale[l][None],
        subln_g=subln_g[l][None],
        conv_w=ssd_conv_w[l], conv_b=ssd_conv_b[l][None],
        dt_bias=jnp.pad(ssd_dt_bias[l].reshape(1, n_dt), ((0, 0), (0, DT_PAD - n_dt))),
        alog_exp=jnp.repeat(alog, SSD_HEAD_DIM, axis=1)[:, None, :],
        alog_small=alog_small,
        expand=_expand_matrices(),
        d_skip=jnp.repeat(ssd_D[l], SSD_HEAD_DIM)[None], norm_g=ssd_norm_g[l][None],
        w_branch=w_branch[l].astype(BF16), w_out=w_out[l].astype(BF16),
        ffn_w_up=ffn_w_up[l].astype(BF16), ffn_conv_w=ffn_conv_w[l], ffn_conv_b=ffn_conv_b[l][None],
        ffn_w_down=ffn_w_down[l].astype(BF16),
    )


def kernel(x_prompt, x_sample, c, cache_attn_k, cache_attn_v, state_ssd, c_ctx, w_mod, b_mod, norm1_g, w_in, pool_w, pool_scale, att_lambda, att_subln_g, ssd_conv_w, ssd_conv_b, ssd_dt_bias, ssd_A_log, ssd_D, ssd_norm_g, w_gate, b_gate, w_branch, w_out, norm2_g, ffn_w_up, ffn_conv_w, ffn_conv_b, ffn_w_down, final_norm_g):
    x = jnp.concatenate([x_prompt.reshape(N_CTX, D_MODEL), x_sample.reshape(N_LAT, D_MODEL)], axis=0)
    cvec = jnp.concatenate([c_ctx[None], c, jnp.zeros((MOD_ROWS - 1 - DEC_BATCH, D_MODEL), F32)], axis=0)
    mod = _modulation(cvec, w_mod, b_mod).reshape(DEPTH, MOD_ROWS, N_MOD, D_MODEL)
    rope_cos, rope_sin = _rope_tables()
    past_k = cache_attn_k.reshape(DEC_BATCH, DEPTH, PAST_LEN, ATT_QK_TOTAL)
    past_v = cache_attn_v.reshape(DEC_BATCH, DEPTH, PAST_LEN, ATT_DIM)
    past_state = state_ssd.reshape(DEC_BATCH, DEPTH, 2, SSD_DIM, SSD_STATE)

    new_k, new_v, new_s = [], [], []
    for l in range(DEPTH):
        p = _layer_weights(l, w_in, w_gate, b_gate, pool_w, pool_scale, att_subln_g, ssd_conv_w,
                           ssd_conv_b, ssd_dt_bias, ssd_A_log, ssd_D, ssd_norm_g, w_branch, w_out,
                           ffn_w_up, ffn_conv_w, ffn_conv_b, ffn_w_down, norm1_g, norm2_g)
        u, q, k, v, z, xbc, dt_raw, gates = _inproj(x, mod[l], p["norm1_g"], p["w_cat"], p["w_gate"],
                                                    p["b_gate"], rope_cos, rope_sin)
        o_pool = _pool_mixer(u, p["pool_w"], p["pool_scale"])
        o_att = _diff_attention(q, k, v, past_k, past_v, att_lambda, p["subln_g"], l)
        o_ssd, st_f, st_b = _ssd_mixer(xbc, dt_raw, z, past_state, l, p)
        x = _merge(x, mod[l], gates, o_pool, o_att, o_ssd, p["w_branch"], p["w_out"])
        x = _conv_ffn(x, mod[l], p["norm2_g"], p["ffn_w_up"], p["ffn_conv_w"], p["ffn_conv_b"],
                      p["ffn_w_down"])
        new_k.append(k[:N_CTX].reshape(BATCH, SEQ, ATT_HEADS, 2, ATT_QK_DIM))
        new_v.append(v[:N_CTX].reshape(BATCH, SEQ, ATT_HEADS, ATT_V_DIM))
        new_s.append(jnp.stack([st_f[:NT_CTX], st_b[:NT_CTX]], axis=1)
                     .reshape(BATCH, 2, SSD_HEADS, SSD_HEAD_DIM, SSD_STATE))

    y = _final_norm(x, final_norm_g[None])
    return (y[:N_CTX].reshape(BATCH, SEQ, D_MODEL), y[N_CTX:].reshape(DEC_BATCH, DEC_SEQ, D_MODEL),
            jnp.stack(new_k, axis=1), jnp.stack(new_v, axis=1), jnp.stack(new_s, axis=1))
```

```python
import functools
import math

import numpy as np
import jax
import jax.numpy as jnp
from jax import lax
from jax.experimental import pallas as pl
from jax.experimental.pallas import tpu as pltpu

D_MODEL = 1024
BATCH = 32
SEQ = 256
DEPTH = 4
DEC_BATCH = 4
DEC_SEQ = 2048
PAST_LEN = 256
GRID_W = 64
EPS = 1e-6
N_MOD = 6

POOL_GROUPS = 4
POOL_WINDOWS = (2, 4, 8, 16)
POOL_DIM = D_MODEL // 2
POOL_GROUP_DIM = POOL_DIM // POOL_GROUPS

ATT_HEADS = 4
ATT_QK_DIM = 64
ATT_V_DIM = 2 * ATT_QK_DIM
ATT_QK_TOTAL = ATT_HEADS * 2 * ATT_QK_DIM
ATT_DIM = ATT_HEADS * ATT_V_DIM
ROPE_BASE = 10000.0

SSD_DIM = D_MODEL // 2
SSD_HEAD_DIM = 64
SSD_HEADS = SSD_DIM // SSD_HEAD_DIM
SSD_GROUPS = 2
SSD_STATE = 128
SSD_CHUNK = 128
SSD_XBC_DIM = SSD_DIM + 2 * SSD_GROUPS * SSD_STATE
SSD_GROUP_DIM = SSD_DIM // SSD_GROUPS

MIX_DIM = POOL_DIM + ATT_DIM + SSD_DIM
FFN_DIM = ((8 * D_MODEL // 3 + 127) // 128) * 128

LANE = 128
SUBLANE = 8
HALO = SUBLANE

TILE = 256
N_CTX = BATCH * SEQ
N_LAT = DEC_BATCH * DEC_SEQ
N_TOK = N_CTX + N_LAT
NT_CTX = N_CTX // TILE
NT = N_TOK // TILE
TILES_PER_LAT = DEC_SEQ // TILE
N_HALO_BLOCKS = N_TOK // HALO
MOD_ROWS = 8
DT_PAD = LANE
FFN_CHUNK = FFN_DIM // 2
VMEM_LIMIT = 56 * 1024 * 1024

ROW_TILE = 512
NRT = N_TOK // ROW_TILE
NRT_CTX = N_CTX // ROW_TILE
ROW_TILES_PER_LAT = DEC_SEQ // ROW_TILE
ROW_GROUPS = ROW_TILE // SUBLANE

KEY_BLOCK = 256
N_KEYS_LAT = PAST_LEN + DEC_SEQ

F32 = jnp.float32
BF16 = jnp.bfloat16

assert SEQ == TILE and DEC_SEQ % TILE == 0 and PAST_LEN == TILE
assert TILE == 2 * SSD_CHUNK and SSD_STATE == LANE and FFN_CHUNK % LANE == 0
assert ROW_TILE % SEQ == 0 and DEC_SEQ % ROW_TILE == 0
assert KEY_BLOCK == PAST_LEN == SEQ and DEC_SEQ % KEY_BLOCK == 0 and KEY_BLOCK == 2 * LANE


def _dot(a, b):
    return jnp.dot(a, b, preferred_element_type=F32)


def _dot_nt(a, b):
    return lax.dot_general(a, b, (((1,), (1,)), ((), ())), preferred_element_type=F32)


def _split3(x):
    x1 = x.astype(BF16)
    r1 = x - x1.astype(F32)
    x2 = r1.astype(BF16)
    x3 = (r1 - x2.astype(F32)).astype(BF16)
    return x1, x2, x3


def _dot_split(a, b_bf16, terms):
    acc = None
    for piece in _split3(a)[:terms]:
        part = _dot(piece, b_bf16)
        acc = part if acc is None else acc + part
    return acc


def _dot_exact_lhs(a_bf16, b):
    b1, b2, b3 = _split3(b)
    return _dot(a_bf16, b1) + _dot(a_bf16, b2) + _dot(a_bf16, b3)


def _silu(x):
    return x * jax.nn.sigmoid(x)


def _tile_pos(t):
    is_ctx = t < NT_CTX
    j = jnp.maximum(t - NT_CTX, 0) % TILES_PER_LAT
    return is_ctx, j


def _lat_batch(t):
    return jnp.maximum(t - NT_CTX, 0) // TILES_PER_LAT


def _prev_halo(t):
    return jnp.maximum(t * (TILE // HALO) - 1, 0)


def _next_halo(t):
    return jnp.minimum((t + 1) * (TILE // HALO), N_HALO_BLOCKS - 1)


def _const_spec(shape):
    nd = len(shape)
    return pl.BlockSpec(shape, lambda *_: (0,) * nd, pipeline_mode=pl.Buffered(1))


def _layer_spec(shape, lidx):
    nd = len(shape)
    return pl.BlockSpec((1,) + tuple(shape), lambda *_: (lidx,) + (0,) * nd,
                        pipeline_mode=pl.Buffered(1))


def _row_spec(cols, fn=lambda i: i):
    return pl.BlockSpec((TILE, cols), lambda i: (fn(i), 0))


def _halo_spec(cols, which, fn=lambda i: i):
    pick = _prev_halo if which == "prev" else _next_halo
    return pl.BlockSpec((HALO, cols), lambda i: (pick(fn(i)), 0))


def _big_row_spec(cols):
    return pl.BlockSpec((ROW_TILE, cols), lambda i: (i, 0))


def _big_halo_spec(cols, which):
    per_tile = ROW_TILE // HALO
    if which == "prev":
        return pl.BlockSpec((HALO, cols), lambda i: (jnp.maximum(i * per_tile - 1, 0), 0))
    return pl.BlockSpec((HALO, cols), lambda i: (jnp.minimum((i + 1) * per_tile, N_HALO_BLOCKS - 1), 0))


def _big_lat_index(i):
    rel = jnp.maximum(i - NRT_CTX, 0)
    return i < NRT_CTX, rel // ROW_TILES_PER_LAT, rel % ROW_TILES_PER_LAT


def _big_mod_spec():
    def idx(i):
        is_ctx, b, _ = _big_lat_index(i)
        return (jnp.where(is_ctx, 0, 1 + b), 0, 0)
    return pl.BlockSpec((1, N_MOD, D_MODEL), idx)


def _params(n_axes=1):
    return pltpu.CompilerParams(dimension_semantics=("arbitrary",) * n_axes,
                                vmem_limit_bytes=VMEM_LIMIT)


def _rmsnorm_rows(x, g):
    ms = jnp.mean(x * x, axis=-1, keepdims=True)
    return x * lax.rsqrt(ms + EPS) * g


MOD_COLS = 1536


def _mod_kernel(c_ref, w_ref, b_ref, o_ref):
    s = _silu(c_ref[...]).astype(BF16)
    o_ref[0] = _dot(s, w_ref[0].astype(BF16)) + b_ref[0]


def _modulation(cvec, w_mod, b_mod):
    nmod = N_MOD * D_MODEL
    return pl.pallas_call(
        _mod_kernel,
        out_shape=jax.ShapeDtypeStruct((DEPTH, MOD_ROWS, nmod), F32),
        grid=(DEPTH, nmod // MOD_COLS),
        in_specs=[pl.BlockSpec((MOD_ROWS, D_MODEL), lambda l, j: (0, 0)),
                  pl.BlockSpec((1, D_MODEL, MOD_COLS), lambda l, j: (l, 0, j)),
                  pl.BlockSpec((1, 1, MOD_COLS), lambda l, j: (l, 0, j))],
        out_specs=pl.BlockSpec((1, MOD_ROWS, MOD_COLS), lambda l, j: (l, 0, j)),
        compiler_params=_params(2),
        name="modulation",
    )(cvec, w_mod, b_mod.reshape(DEPTH, 1, nmod))


_C_POOL = 0
_C_Q = _C_POOL + POOL_DIM
_C_K = _C_Q + ATT_QK_TOTAL
_C_V = _C_K + ATT_QK_TOTAL
_C_Z = _C_V + ATT_DIM
_C_XBC = _C_Z + SSD_DIM
_C_DT = _C_XBC + SSD_XBC_DIM
IN_DIM = _C_DT + 2 * SSD_HEADS
ROPE_SWAP = ATT_QK_DIM // 4
Q_PRESCALE = ATT_QK_DIM ** -0.5 * math.log2(math.e)


def _rope_block(xb, cos, sin):
    lane = lax.broadcasted_iota(jnp.int32, xb.shape, 1)
    first_half = (lane % (2 * ROPE_SWAP)) < ROPE_SWAP
    partner = jnp.where(first_half,
                        pltpu.roll(xb, LANE - ROPE_SWAP, axis=1),
                        pltpu.roll(xb, ROPE_SWAP, axis=1))
    return xb * cos + partner * sin


def _modulated_norm(x, g, mod_ref, shift_row):
    return (_rmsnorm_rows(x, g) * (1.0 + mod_ref[0, shift_row + 1:shift_row + 2, :])
            + mod_ref[0, shift_row:shift_row + 1, :])


def _inproj_kernel(x_ref, mod_ref, g_ref, w_ref, wdt_ref, cos_ref, sin_ref,
                   u_ref, q_ref, k_ref, v_ref, kb_ref, vb_ref, z_ref, xbc_ref, dt_ref):
    i = pl.program_id(0)
    hb = _modulated_norm(x_ref[...], g_ref[0], mod_ref, 0).astype(BF16)
    w = w_ref.at[0]
    u_ref[...] = _dot(hb, w[:, _C_POOL:_C_Q])
    v = _dot(hb, w[:, _C_V:_C_Z])
    v_ref[...] = v
    vb_ref[...] = v.astype(BF16)
    z_ref[...] = _dot(hb, w[:, _C_Z:_C_XBC])
    xbc_ref[...] = _dot(hb, w[:, _C_XBC:_C_DT])
    dt_ref[...] = _dot(hb, wdt_ref[0])
    q = _dot(hb, w[:, _C_Q:_C_K])
    k = _dot(hb, w[:, _C_K:_C_V])

    @pl.when(i < NRT_CTX)
    def _():
        q_ref[...] = (q * Q_PRESCALE).astype(BF16)
        k_ref[...] = k
        kb_ref[...] = k.astype(BF16)

    @pl.when(i >= NRT_CTX)
    def _():
        cos = cos_ref[...]
        sin = sin_ref[...]
        for blk in range(ATT_QK_TOTAL // LANE):
            sl = slice(blk * LANE, (blk + 1) * LANE)
            q_ref[:, sl] = (_rope_block(q[:, sl], cos, sin) * Q_PRESCALE).astype(BF16)
            kr = _rope_block(k[:, sl], cos, sin)
            k_ref[:, sl] = kr
            kb_ref[:, sl] = kr.astype(BF16)


def _inproj(x, mod_l, norm_g, w_in, w_dt, rope_cos, rope_sin, lidx):
    outs = [(POOL_DIM, F32), (ATT_QK_TOTAL, BF16), (ATT_QK_TOTAL, F32), (ATT_DIM, F32),
            (ATT_QK_TOTAL, BF16), (ATT_DIM, BF16),
            (SSD_DIM, F32), (SSD_XBC_DIM, F32), (DT_PAD, F32)]
    rope_idx = lambda i: (_big_lat_index(i)[2], 0)
    return pl.pallas_call(
        _inproj_kernel,
        out_shape=[jax.ShapeDtypeStruct((N_TOK, c), d) for c, d in outs],
        grid=(NRT,),
        in_specs=[_big_row_spec(D_MODEL), _big_mod_spec(),
                  _layer_spec((1, D_MODEL), lidx),
                  _layer_spec((D_MODEL, IN_DIM), lidx),
                  _layer_spec((D_MODEL, DT_PAD), lidx),
                  pl.BlockSpec((ROW_TILE, LANE), rope_idx),
                  pl.BlockSpec((ROW_TILE, LANE), rope_idx)],
        out_specs=[_big_row_spec(c) for c, _ in outs],
        compiler_params=_params(),
        name="inproj",
    )(x, mod_l, norm_g, w_in, w_dt, rope_cos, rope_sin)


def _fill_halo_scratch(scr, prev_ref, cur, next_ref, first, last):
    scr[0:HALO, :] = jnp.where(first, 0.0, prev_ref[...])
    scr[HALO:HALO + TILE, :] = cur
    scr[HALO + TILE:, :] = jnp.where(last, 0.0, next_ref[...])


def _pool_kernel(u_ref, up_ref, un_ref, w_ref, s_ref, o_ref, scr):
    i = pl.program_id(0)
    is_ctx, j = _tile_pos(i)
    first = jnp.logical_or(is_ctx, j == 0)
    last = jnp.logical_or(is_ctx, j == TILES_PER_LAT - 1)
    _fill_halo_scratch(scr, up_ref, u_ref[...], un_ref, first, last)
    seq_len = jnp.where(is_ctx, SEQ, DEC_SEQ)
    pos = jnp.where(is_ctx, 0, j * TILE) + lax.broadcasted_iota(jnp.int32, (TILE, 1), 0)
    for g, win in enumerate(POOL_WINDOWS):
        half = win // 2
        sl = slice(g * POOL_GROUP_DIM, (g + 1) * POOL_GROUP_DIM)
        acc = scr[HALO - half:HALO - half + TILE, sl]
        for k in range(1 - half, half):
            acc = acc + scr[HALO + k:HALO + k + TILE, sl]
        lo = jnp.maximum(pos - half, 0)
        hi = jnp.minimum(pos + half - 1, seq_len - 1)
        count = (hi - lo + 1).astype(F32)
        pooled = acc / count - u_ref[:, sl]
        mixed = _dot(pooled.astype(BF16), w_ref[0, g]) * s_ref[0, :, sl]
        o_ref[:, sl] = mixed.astype(BF16)


def _pool_mixer(u, pool_w, pool_scale, lidx):
    return pl.pallas_call(
        _pool_kernel,
        out_shape=jax.ShapeDtypeStruct((N_TOK, POOL_DIM), BF16),
        grid=(NT,),
        in_specs=[_row_spec(POOL_DIM), _halo_spec(POOL_DIM, "prev"), _halo_spec(POOL_DIM, "next"),
                  _layer_spec((POOL_GROUPS, POOL_GROUP_DIM, POOL_GROUP_DIM), lidx),
                  _layer_spec((1, POOL_DIM), lidx)],
        out_specs=_row_spec(POOL_DIM),
        scratch_shapes=[pltpu.VMEM((TILE + 2 * HALO, POOL_DIM), F32)],
        compiler_params=_params(),
        name="pool_mixer",
    )(u, u, u, pool_w, pool_scale)


def _attn_kernel(q_ref, kc_ref, vc_ref, kl_ref, vl_ref, kp_ref, vp_ref, lam_ref, g_ref, o_ref,
                 s_scr, *, lam_init):
    i = pl.program_id(0)
    lp = lam_ref[0]
    lam = (jnp.exp(jnp.sum(lp[0:1] * lp[1:2], axis=-1, keepdims=True))
           - jnp.exp(jnp.sum(lp[2:3] * lp[3:4], axis=-1, keepdims=True)) + lam_init)

    def run(key_blocks):
        for h in range(ATT_HEADS):
            sl = slice(h * LANE, (h + 1) * LANE)
            qh = q_ref[:, sl]
            lane = lax.broadcasted_iota(jnp.int32, qh.shape, 1)
            zero = jnp.zeros_like(qh)
            q2 = jnp.concatenate([jnp.where(lane < ATT_QK_DIM, qh, zero),
                                  jnp.where(lane >= ATT_QK_DIM, qh, zero)], axis=0)
            run_max = None
            for b, (kr, _, r0) in enumerate(key_blocks):
                s = _dot_nt(q2, kr[r0:r0 + KEY_BLOCK, sl])
                s_scr[:, b * KEY_BLOCK:(b + 1) * KEY_BLOCK] = s
                blk_max = jnp.maximum(s[:, 0:LANE], s[:, LANE:KEY_BLOCK])
                run_max = blk_max if run_max is None else jnp.maximum(run_max, blk_max)
            m = jnp.max(run_max, axis=-1, keepdims=True)
            run_sum = None
            acc = None
            for b, (_, vr, r0) in enumerate(key_blocks):
                p = jnp.exp2(s_scr[:, b * KEY_BLOCK:(b + 1) * KEY_BLOCK] - m)
                blk_sum = p[:, 0:LANE] + p[:, LANE:KEY_BLOCK]
                run_sum = blk_sum if run_sum is None else run_sum + blk_sum
                part = _dot(p.astype(BF16), vr[r0:r0 + KEY_BLOCK, sl])
                acc = part if acc is None else acc + part
            o2 = acc / jnp.sum(run_sum, axis=-1, keepdims=True)
            oh = o2[0:TILE] - lam * o2[TILE:]
            oh = _rmsnorm_rows(oh, g_ref[0]) * (1.0 - lam_init)
            o_ref[:, sl] = oh.astype(BF16)

    @pl.when(i < NT_CTX)
    def _():
        run([(kc_ref, vc_ref, 0)])

    @pl.when(i >= NT_CTX)
    def _():
        run([(kp_ref.at[0, 0], vp_ref.at[0, 0], 0)]
            + [(kl_ref, vl_ref, r0) for r0 in range(0, DEC_SEQ, KEY_BLOCK)])


def _diff_attention(q, kb, vb, past_k, past_v, att_lambda, subln_g, lidx):
    lam_init = 0.8 - 0.6 * math.exp(-0.3 * lidx)
    ctx_idx = lambda i: (jnp.minimum(i, NT_CTX - 1), 0)
    lat_idx = lambda i: (N_CTX // DEC_SEQ + _lat_batch(i), 0)
    past_idx = lambda i: (_lat_batch(i), lidx, 0, 0)
    return pl.pallas_call(
        functools.partial(_attn_kernel, lam_init=lam_init),
        out_shape=jax.ShapeDtypeStruct((N_TOK, ATT_DIM), BF16),
        grid=(NT,),
        in_specs=[_row_spec(ATT_QK_TOTAL),
                  pl.BlockSpec((TILE, ATT_QK_TOTAL), ctx_idx),
                  pl.BlockSpec((TILE, ATT_DIM), ctx_idx),
                  pl.BlockSpec((DEC_SEQ, ATT_QK_TOTAL), lat_idx),
                  pl.BlockSpec((DEC_SEQ, ATT_DIM), lat_idx),
                  pl.BlockSpec((1, 1, PAST_LEN, ATT_QK_TOTAL), past_idx),
                  pl.BlockSpec((1, 1, PAST_LEN, ATT_DIM), past_idx),
                  _layer_spec((4, ATT_QK_DIM), lidx),
                  _layer_spec((1, ATT_V_DIM), lidx)],
        out_specs=_row_spec(ATT_DIM),
        scratch_shapes=[pltpu.VMEM((2 * TILE, N_KEYS_LAT), F32)],
        compiler_params=_params(),
        name="diff_attention",
    )(q, kb, vb, kb, vb, past_k, past_v, att_lambda, subln_g)


def _ssd_scan_tile(direction, t, xc_scr, dt_ref, h0_ref, dtb_ref, asmall_ref, e_ref, st_ref,
                   state_scr, y_scr):
    is_ctx, j = _tile_pos(t)
    seq_begins = (j == 0) if direction == 0 else (j == TILES_PER_LAT - 1)

    @pl.when(is_ctx)
    def _():
        state_scr[...] = jnp.zeros_like(state_scr)

    @pl.when(jnp.logical_and(jnp.logical_not(is_ctx), seq_begins))
    def _():
        state_scr[...] = h0_ref[0, 0, 0].T

    a_small = -jnp.exp(asmall_ref[...])
    row = lax.broadcasted_iota(jnp.int32, (SSD_CHUNK, SSD_CHUNK), 0)
    col = lax.broadcasted_iota(jnp.int32, (SSD_CHUNK, SSD_CHUNK), 1)
    keep = (col <= row) if direction == 0 else (col >= row)
    tri = jnp.where(keep, 1.0, 0.0).astype(BF16)
    pair_lane = lax.broadcasted_iota(jnp.int32, (SSD_CHUNK, LANE), 1)
    edge = SSD_CHUNK - 1 if direction == 0 else 0
    heads_per_group = SSD_HEADS // SSD_GROUPS

    chunks = (0, 1) if direction == 0 else (1, 0)
    for c in chunks:
        rows = slice(c * SSD_CHUNK, (c + 1) * SSD_CHUNK)
        dt = jax.nn.softplus(dt_ref[rows, :] + dtb_ref[...])
        dt_exp = _dot_split(dt, e_ref[...], 2)
        acs_s = _dot_exact_lhs(tri, dt * a_small)
        acs_exp = _dot_split(acs_s, e_ref[...], 3)
        acs_st = acs_s.T
        xdt = xc_scr[rows, 0:SSD_DIM] * dt_exp
        total = acs_exp[edge:edge + 1, :]
        xw = (xdt * jnp.exp(total - acs_exp)).astype(BF16)
        decay_from_entry = jnp.exp(acs_exp)
        for g in range(SSD_GROUPS):
            gs = slice(g * SSD_GROUP_DIM, (g + 1) * SSD_GROUP_DIM)
            b_g = xc_scr[rows, SSD_DIM + g * SSD_STATE:SSD_DIM + (g + 1) * SSD_STATE]
            c_g = xc_scr[rows, SSD_DIM + (SSD_GROUPS + g) * SSD_STATE:
                         SSD_DIM + (SSD_GROUPS + g + 1) * SSD_STATE].astype(BF16)
            cb = _dot_nt(c_g, b_g.astype(BF16))
            h_in = state_scr[:, gs]
            y_g = _dot(c_g, h_in.astype(BF16)) * decay_from_entry[:, gs]
            pairs = []
            for pr in range(heads_per_group // 2):
                ps = slice(g * SSD_GROUP_DIM + pr * LANE, g * SSD_GROUP_DIM + (pr + 1) * LANE)
                x_pair = xdt[:, ps].astype(BF16)
                halves = []
                for hh in range(2):
                    hcol = direction * SSD_HEADS + g * heads_per_group + pr * 2 + hh
                    diff = acs_s[:, hcol:hcol + 1] - acs_st[hcol:hcol + 1, :]
                    decay = jnp.where(keep, jnp.exp(diff), 0.0)
                    halves.append(_dot((cb * decay).astype(BF16), x_pair))
                pairs.append(jnp.where(pair_lane < SSD_HEAD_DIM, halves[0], halves[1]))
            y_g = y_g + jnp.concatenate(pairs, axis=1)
            y_scr[rows, gs] = y_g
            state_scr[:, gs] = (h_in * jnp.exp(total[:, gs])
                                + _dot(b_g.T.astype(BF16), xw[:, gs]))
    st_ref[0] = state_scr[...].T


def _ssd_fwd_kernel(xbc_ref, xp_ref, xn_ref, dt_ref, h0_ref, cw_ref, cb_ref, dtb_ref, asmall_ref, e_ref,
                    y_ref, xc_ref, st_ref, conv_scr, state_scr):
    t = pl.program_id(0)
    is_ctx, j = _tile_pos(t)
    first = jnp.logical_or(is_ctx, j == 0)
    last = jnp.logical_or(is_ctx, j == TILES_PER_LAT - 1)
    _fill_halo_scratch(conv_scr, xp_ref, xbc_ref[...], xn_ref, first, last)
    conv = (cw_ref[0, 0:1, :] * conv_scr[HALO - 1:HALO - 1 + TILE, :]
            + cw_ref[0, 1:2, :] * conv_scr[HALO:HALO + TILE, :]
            + cw_ref[0, 2:3, :] * conv_scr[HALO + 1:HALO + 1 + TILE, :] + cb_ref[0])
    xc_ref[...] = _silu(conv)
    _ssd_scan_tile(0, t, xc_ref, dt_ref, h0_ref, dtb_ref, asmall_ref, e_ref, st_ref, state_scr, y_ref)


def _ssd_bwd_kernel(xc_ref, dt_ref, h0_ref, dtb_ref, asmall_ref, e_ref, yf_ref, z_ref, dskip_ref, ng_ref,
                    o_ref, st_ref, state_scr, y_scr):
    t = NT - 1 - pl.program_id(0)
    _ssd_scan_tile(1, t, xc_ref, dt_ref, h0_ref, dtb_ref, asmall_ref, e_ref, st_ref, state_scr, y_scr)
    y = yf_ref[...] + y_scr[...] + dskip_ref[...] * xc_ref[:, 0:SSD_DIM]
    yz = y * _silu(z_ref[...])
    for g in range(SSD_GROUPS):
        gs = slice(g * SSD_GROUP_DIM, (g + 1) * SSD_GROUP_DIM)
        o_ref[:, gs] = _rmsnorm_rows(yz[:, gs], ng_ref[0, :, gs]).astype(BF16)


def _ssd_mixer(xbc, dt_raw, z, state_ssd, lidx, conv_w, conv_b, norm_g, p):
    st_shape = jax.ShapeDtypeStruct((NT, SSD_DIM, SSD_STATE), F32)
    rev = lambda i: NT - 1 - i
    state_scratch = pltpu.VMEM((SSD_STATE, SSD_DIM), F32)

    def scan_specs(order, direction):
        h0_idx = lambda i: (_lat_batch(order(i)), lidx, direction, 0, 0)
        return [_row_spec(DT_PAD, order),
                pl.BlockSpec((1, 1, 1, SSD_DIM, SSD_STATE), h0_idx),
                _const_spec((1, DT_PAD)), _const_spec((1, DT_PAD)), _const_spec((DT_PAD, SSD_DIM))]

    def scan_args(direction):
        return [dt_raw, state_ssd, p["dt_bias"], p["alog_small"][direction], p["expand"][direction]]

    fwd = lambda i: i
    y_f, xc, st_f = pl.pallas_call(
        _ssd_fwd_kernel,
        out_shape=[jax.ShapeDtypeStruct((N_TOK, SSD_DIM), F32),
                   jax.ShapeDtypeStruct((N_TOK, SSD_XBC_DIM), F32), st_shape],
        grid=(NT,),
        in_specs=([_row_spec(SSD_XBC_DIM), _halo_spec(SSD_XBC_DIM, "prev"), _halo_spec(SSD_XBC_DIM, "next")]
                  + scan_specs(fwd, 0)[:2]
                  + [_layer_spec((3, SSD_XBC_DIM), lidx), _layer_spec((1, SSD_XBC_DIM), lidx)]
                  + scan_specs(fwd, 0)[2:]),
        out_specs=[_row_spec(SSD_DIM), _row_spec(SSD_XBC_DIM),
                   pl.BlockSpec((1, SSD_DIM, SSD_STATE), lambda i: (i, 0, 0))],
        scratch_shapes=[pltpu.VMEM((TILE + 2 * HALO, SSD_XBC_DIM), F32), state_scratch],
        compiler_params=_params(),
        name="ssd_forward",
    )(xbc, xbc, xbc, *scan_args(0)[:2], conv_w, conv_b, *scan_args(0)[2:])

    out, st_b = pl.pallas_call(
        _ssd_bwd_kernel,
        out_shape=[jax.ShapeDtypeStruct((N_TOK, SSD_DIM), BF16), st_shape],
        grid=(NT,),
        in_specs=([_row_spec(SSD_XBC_DIM, rev)] + scan_specs(rev, 1)
                  + [_row_spec(SSD_DIM, rev), _row_spec(SSD_DIM, rev),
                     _const_spec((1, SSD_DIM)), _layer_spec((1, SSD_DIM), lidx)]),
        out_specs=[_row_spec(SSD_DIM, rev),
                   pl.BlockSpec((1, SSD_DIM, SSD_STATE), lambda i: (rev(i), 0, 0))],
        scratch_shapes=[state_scratch, pltpu.VMEM((TILE, SSD_DIM), F32)],
        compiler_params=_params(),
        name="ssd_backward",
    )(xc, *scan_args(1), y_f, z, p["d_skip"], norm_g)
    return out, st_f, st_b


def _merge_kernel(x_ref, mod_ref, g_ref, wg_ref, bg_ref, op_ref, oa_ref, os_ref, wb_ref, wo_ref,
                  o_ref):
    x = x_ref[...]
    hb = _modulated_norm(x, g_ref[0], mod_ref, 0).astype(BF16)
    wg = wg_ref.at[0]
    wb = wb_ref.at[0]
    branches = ((op_ref, 0, POOL_DIM), (oa_ref, POOL_DIM, ATT_DIM), (os_ref, POOL_DIM + ATT_DIM, SSD_DIM))
    merged = None
    for n, (b_ref, row0, width) in enumerate(branches):
        cols = slice(n * D_MODEL, (n + 1) * D_MODEL)
        gate = jax.nn.sigmoid(_dot(hb, wg[:, cols]) + bg_ref[0, :, cols])
        term = gate * _dot(b_ref[...], wb[row0:row0 + width, :])
        merged = term if merged is None else merged + term
    o_ref[...] = x + mod_ref[0, 2:3, :] * _dot(merged.astype(BF16), wo_ref[0])


def _merge(x, mod_l, norm_g, w_gate, b_gate, o_pool, o_att, o_ssd, w_branch, w_out, lidx):
    return pl.pallas_call(
        _merge_kernel,
        out_shape=jax.ShapeDtypeStruct((N_TOK, D_MODEL), F32),
        grid=(NRT,),
        in_specs=[_big_row_spec(D_MODEL), _big_mod_spec(),
                  _layer_spec((1, D_MODEL), lidx),
                  _layer_spec((D_MODEL, 3 * D_MODEL), lidx), _layer_spec((1, 3 * D_MODEL), lidx),
                  _big_row_spec(POOL_DIM), _big_row_spec(ATT_DIM), _big_row_spec(SSD_DIM),
                  _layer_spec((MIX_DIM, D_MODEL), lidx), _layer_spec((D_MODEL, D_MODEL), lidx)],
        out_specs=_big_row_spec(D_MODEL),
        compiler_params=_params(),
        name="merge",
    )(x, mod_l, norm_g, w_gate, b_gate, o_pool, o_att, o_ssd, w_branch, w_out)


def _ffn_kernel(x_ref, xp_ref, xn_ref, mod_ref, g_ref, wu_ref, cw_ref, cb_ref, wd_ref, fg_ref, o_ref,
                h_scr, cols_scr, *, final):
    i = pl.program_id(0)
    is_ctx, _, j = _big_lat_index(i)
    first = jnp.logical_or(is_ctx, j == 0)
    last = jnp.logical_or(is_ctx, j == ROW_TILES_PER_LAT - 1)
    seq_len = jnp.where(is_ctx, SEQ, DEC_SEQ)
    tile_pos = jnp.where(is_ctx, 0, j * ROW_TILE)
    n_col = D_MODEL // LANE

    g = g_ref[0]
    h_nat = _modulated_norm(x_ref[...], g, mod_ref, 3)
    for cb in range(n_col):
        cols_scr[cb] = h_nat[:, cb * LANE:(cb + 1) * LANE]
    for r in range(ROW_GROUPS):
        for cb in range(n_col):
            h_scr[r * SUBLANE:(r + 1) * SUBLANE, cb * LANE:(cb + 1) * LANE] = (
                cols_scr[cb, pl.ds(r, SUBLANE, stride=ROW_GROUPS), :])
    h_scr[ROW_TILE:ROW_TILE + HALO, :] = jnp.where(first, 0.0, _modulated_norm(xp_ref[...], g, mod_ref, 3))
    h_scr[ROW_TILE + HALO:, :] = jnp.where(last, 0.0, _modulated_norm(xn_ref[...], g, mod_ref, 3))
    hb = h_scr[...].astype(BF16)

    sub = lax.broadcasted_iota(jnp.int32, (SUBLANE, 1), 0)
    pos_first = tile_pos + sub * ROW_GROUPS
    pos_last = pos_first + (ROW_GROUPS - 1)
    at_seq_start = (pos_first % seq_len) == 0
    at_seq_end = (pos_last % seq_len) == seq_len - 1
    wu = wu_ref.at[0]

    def conv_half(col0):
        cols = slice(col0, col0 + FFN_CHUNK)
        up = _dot(hb, wu[:, cols])
        a = up[0:ROW_TILE]
        halo_prev = up[ROW_TILE + HALO - 1:ROW_TILE + HALO]
        halo_next = up[ROW_TILE + HALO:ROW_TILE + HALO + 1]
        wrap_prev = pltpu.roll(a[ROW_TILE - SUBLANE:ROW_TILE], 1, axis=0)
        wrap_prev = jnp.where(sub == 0, halo_prev, wrap_prev)
        wrap_prev = jnp.where(at_seq_start, 0.0, wrap_prev)
        wrap_next = pltpu.roll(a[0:SUBLANE], SUBLANE - 1, axis=0)
        wrap_next = jnp.where(sub == SUBLANE - 1, halo_next, wrap_next)
        wrap_next = jnp.where(at_seq_end, 0.0, wrap_next)
        before = jnp.concatenate([wrap_prev, a[0:ROW_TILE - SUBLANE]], axis=0)
        after = jnp.concatenate([a[SUBLANE:ROW_TILE], wrap_next], axis=0)
        return (cw_ref[0, 0:1, cols] * before + cw_ref[0, 1:2, cols] * a
                + cw_ref[0, 2:3, cols] * after + cb_ref[0, :, cols])

    acc = None
    for c in range(FFN_DIM // FFN_CHUNK):
        a_gate = conv_half(c * FFN_CHUNK)
        a_val = conv_half(FFN_DIM + c * FFN_CHUNK)
        act = (_silu(a_gate) * a_val).astype(BF16)
        part = _dot(act, wd_ref[0, c * FFN_CHUNK:(c + 1) * FFN_CHUNK, :])
        acc = part if acc is None else acc + part
    upd = mod_ref[0, 5:6, :] * acc
    for cb in range(n_col):
        cols_scr[cb] = upd[:, cb * LANE:(cb + 1) * LANE]
    groups_per_sub = ROW_GROUPS // SUBLANE
    for q in range(ROW_GROUPS):
        start = (q % groups_per_sub) * ROW_GROUPS + q // groups_per_sub
        rows = slice(q * SUBLANE, (q + 1) * SUBLANE)
        for cb in range(n_col):
            cs = slice(cb * LANE, (cb + 1) * LANE)
            o_ref[rows, cs] = x_ref[rows, cs] + cols_scr[cb, pl.ds(start, SUBLANE, stride=SUBLANE), :]
    if final:
        o_ref[...] = _rmsnorm_rows(o_ref[...], fg_ref[...])


def _conv_ffn(x, mod_l, norm_g, w_up, conv_w, conv_b, w_down, final_g, lidx, final):
    return pl.pallas_call(
        functools.partial(_ffn_kernel, final=final),
        out_shape=jax.ShapeDtypeStruct((N_TOK, D_MODEL), F32),
        grid=(NRT,),
        in_specs=[_big_row_spec(D_MODEL), _big_halo_spec(D_MODEL, "prev"), _big_halo_spec(D_MODEL, "next"),
                  _big_mod_spec(),
                  _layer_spec((1, D_MODEL), lidx),
                  _layer_spec((D_MODEL, 2 * FFN_DIM), lidx), _layer_spec((3, 2 * FFN_DIM), lidx),
                  _layer_spec((1, 2 * FFN_DIM), lidx), _layer_spec((FFN_DIM, D_MODEL), lidx),
                  _const_spec((1, D_MODEL))],
        out_specs=_big_row_spec(D_MODEL),
        scratch_shapes=[pltpu.VMEM((ROW_TILE + 2 * HALO, D_MODEL), F32),
                        pltpu.VMEM((D_MODEL // LANE, ROW_TILE, LANE), F32)],
        compiler_params=_params(),
        name="conv_ffn",
    )(x, x, x, mod_l, norm_g, w_up, conv_w, conv_b, w_down, final_g)


def _rope_tables():
    rows = DEC_SEQ // GRID_W
    row = jnp.repeat(jnp.arange(rows), GRID_W).astype(F32)
    col = jnp.tile(jnp.arange(GRID_W), rows).astype(F32)
    half = ATT_QK_DIM // 2
    inv = ROPE_BASE ** (-jnp.arange(0, half, 2, dtype=F32) / half)
    ar, ac = row[:, None] * inv, col[:, None] * inv
    cos = jnp.concatenate([jnp.cos(ar), jnp.cos(ar), jnp.cos(ac), jnp.cos(ac)], axis=-1)
    sin = jnp.concatenate([-jnp.sin(ar), jnp.sin(ar), -jnp.sin(ac), jnp.sin(ac)], axis=-1)
    reps = LANE // ATT_QK_DIM
    return jnp.tile(cos, (1, reps)), jnp.tile(sin, (1, reps))


def _expand_matrices():
    e = np.zeros((2, DT_PAD, SSD_DIM), np.float32)
    for d in range(2):
        for h in range(SSD_HEADS):
            e[d, d * SSD_HEADS + h, h * SSD_HEAD_DIM:(h + 1) * SSD_HEAD_DIM] = 1.0
    return jnp.asarray(e, BF16)


def _ssd_small_params(l, ssd_dt_bias, ssd_A_log, ssd_D):
    n_dt = 2 * SSD_HEADS
    alog = ssd_A_log[l]
    alog_small = jnp.zeros((2, 1, DT_PAD), F32)
    for d in range(2):
        alog_small = alog_small.at[d, 0, d * SSD_HEADS:(d + 1) * SSD_HEADS].set(alog[d])
    return dict(
        dt_bias=jnp.pad(ssd_dt_bias[l].reshape(1, n_dt), ((0, 0), (0, DT_PAD - n_dt))),
        alog_small=alog_small,
        expand=_expand_matrices(),
        d_skip=jnp.repeat(ssd_D[l], SSD_HEAD_DIM)[None],
    )


def kernel(x_prompt, x_sample, c, cache_attn_k, cache_attn_v, state_ssd, c_ctx, w_mod, b_mod, norm1_g, w_in, pool_w, pool_scale, att_lambda, att_subln_g, ssd_conv_w, ssd_conv_b, ssd_dt_bias, ssd_A_log, ssd_D, ssd_norm_g, w_gate, b_gate, w_branch, w_out, norm2_g, ffn_w_up, ffn_conv_w, ffn_conv_b, ffn_w_down, final_norm_g):
    x = jnp.concatenate([x_prompt.reshape(N_CTX, D_MODEL), x_sample.reshape(N_LAT, D_MODEL)], axis=0)
    cvec = jnp.concatenate([c_ctx[None], c, jnp.zeros((MOD_ROWS - 1 - DEC_BATCH, D_MODEL), F32)], axis=0)
    mod = _modulation(cvec, w_mod, b_mod).reshape(DEPTH, MOD_ROWS, N_MOD, D_MODEL)
    rope_cos, rope_sin = _rope_tables()
    past_k = cache_attn_k.reshape(DEC_BATCH, DEPTH, PAST_LEN, ATT_QK_TOTAL).astype(BF16)
    past_v = cache_attn_v.reshape(DEC_BATCH, DEPTH, PAST_LEN, ATT_DIM).astype(BF16)
    past_state = state_ssd.reshape(DEC_BATCH, DEPTH, 2, SSD_DIM, SSD_STATE)

    n_dt = 2 * SSD_HEADS
    w_in_b = w_in.astype(BF16)
    w_dt_b = jnp.pad(w_in[:, :, _C_DT:], ((0, 0), (0, 0), (0, DT_PAD - n_dt))).astype(BF16)
    w_gate_b, w_branch_b, w_out_b = w_gate.astype(BF16), w_branch.astype(BF16), w_out.astype(BF16)
    w_up_b, w_down_b, pool_w_b = ffn_w_up.astype(BF16), ffn_w_down.astype(BF16), pool_w.astype(BF16)
    row = lambda a: a[:, None, :]
    norm1_r, norm2_r, b_gate_r, pool_scale_r = row(norm1_g), row(norm2_g), row(b_gate), row(pool_scale)
    subln_r, conv_b_r, ssd_norm_r, ffn_conv_b_r = row(att_subln_g), row(ssd_conv_b), row(ssd_norm_g), row(ffn_conv_b)
    final_g = final_norm_g[None]

    new_k, new_v, new_s = [], [], []
    for l in range(DEPTH):
        u, q, k, v, kb, vb, z, xbc, dt_raw = _inproj(x, mod[l], norm1_r, w_in_b, w_dt_b,
                                                     rope_cos, rope_sin, l)
        o_pool = _pool_mixer(u, pool_w_b, pool_scale_r, l)
        o_att = _diff_attention(q, kb, vb, past_k, past_v, att_lambda, subln_r, l)
        o_ssd, st_f, st_b = _ssd_mixer(xbc, dt_raw, z, past_state, l, ssd_conv_w, conv_b_r, ssd_norm_r,
                                       _ssd_small_params(l, ssd_dt_bias, ssd_A_log, ssd_D))
        x = _merge(x, mod[l], norm1_r, w_gate_b, b_gate_r, o_pool, o_att, o_ssd, w_branch_b, w_out_b, l)
        x = _conv_ffn(x, mod[l], norm2_r, w_up_b, ffn_conv_w, ffn_conv_b_r, w_down_b, final_g, l,
                      final=(l == DEPTH - 1))
        new_k.append(k[:N_CTX].reshape(BATCH, SEQ, ATT_HEADS, 2, ATT_QK_DIM))
        new_v.append(v[:N_CTX].reshape(BATCH, SEQ, ATT_HEADS, ATT_V_DIM))
        new_s.append(jnp.stack([st_f[:NT_CTX], st_b[:NT_CTX]], axis=1)
                     .reshape(BATCH, 2, SSD_HEADS, SSD_HEAD_DIM, SSD_STATE))

    return (x[:N_CTX].reshape(BATCH, SEQ, D_MODEL), x[N_CTX:].reshape(DEC_BATCH, DEC_SEQ, D_MODEL),
            jnp.stack(new_k, axis=1), jnp.stack(new_v, axis=1), jnp.stack(new_s, axis=1))
```

```python
import functools
import math

import numpy as np
import jax
import jax.numpy as jnp
from jax import lax
from jax.experimental import pallas as pl
from jax.experimental.pallas import tpu as pltpu

D_MODEL = 1024
BATCH = 32
SEQ = 256
DEPTH = 4
DEC_BATCH = 4
DEC_SEQ = 2048
PAST_LEN = 256
GRID_W = 64
EPS = 1e-6
N_MOD = 6

POOL_GROUPS = 4
POOL_WINDOWS = (2, 4, 8, 16)
POOL_DIM = D_MODEL // 2
POOL_GROUP_DIM = POOL_DIM // POOL_GROUPS

ATT_HEADS = 4
ATT_QK_DIM = 64
ATT_V_DIM = 2 * ATT_QK_DIM
ATT_QK_TOTAL = ATT_HEADS * 2 * ATT_QK_DIM
ATT_DIM = ATT_HEADS * ATT_V_DIM
ROPE_BASE = 10000.0

SSD_DIM = D_MODEL // 2
SSD_HEAD_DIM = 64
SSD_HEADS = SSD_DIM // SSD_HEAD_DIM
SSD_GROUPS = 2
SSD_STATE = 128
SSD_CHUNK = 128
SSD_XBC_DIM = SSD_DIM + 2 * SSD_GROUPS * SSD_STATE
SSD_GROUP_DIM = SSD_DIM // SSD_GROUPS

MIX_DIM = POOL_DIM + ATT_DIM + SSD_DIM
FFN_DIM = ((8 * D_MODEL // 3 + 127) // 128) * 128

LANE = 128
SUBLANE = 8
HALO = SUBLANE

TILE = 256
N_CTX = BATCH * SEQ
N_LAT = DEC_BATCH * DEC_SEQ
N_TOK = N_CTX + N_LAT
NT_CTX = N_CTX // TILE
NT = N_TOK // TILE
TILES_PER_LAT = DEC_SEQ // TILE
N_HALO_BLOCKS = N_TOK // HALO
MOD_ROWS = 8
DT_PAD = LANE
FFN_CHUNK = FFN_DIM // 2
VMEM_LIMIT = 56 * 1024 * 1024

ROW_TILE = 512
NRT = N_TOK // ROW_TILE
NRT_CTX = N_CTX // ROW_TILE
ROW_TILES_PER_LAT = DEC_SEQ // ROW_TILE
ROW_GROUPS = ROW_TILE // SUBLANE

KEY_BLOCK = 256
N_KEYS_LAT = PAST_LEN + DEC_SEQ

F32 = jnp.float32
BF16 = jnp.bfloat16

assert SEQ == TILE and DEC_SEQ % TILE == 0 and PAST_LEN == TILE
assert TILE == 2 * SSD_CHUNK and SSD_STATE == LANE and FFN_CHUNK % LANE == 0
assert ROW_TILE % SEQ == 0 and DEC_SEQ % ROW_TILE == 0
assert KEY_BLOCK == PAST_LEN == SEQ and DEC_SEQ % KEY_BLOCK == 0 and KEY_BLOCK == 2 * LANE


def _dot(a, b):
    return jnp.dot(a, b, preferred_element_type=F32)


def _dot_nt(a, b):
    return lax.dot_general(a, b, (((1,), (1,)), ((), ())), preferred_element_type=F32)


def _split3(x):
    x1 = x.astype(BF16)
    r1 = x - x1.astype(F32)
    x2 = r1.astype(BF16)
    x3 = (r1 - x2.astype(F32)).astype(BF16)
    return x1, x2, x3


def _dot_split(a, b_bf16, terms):
    acc = None
    for piece in _split3(a)[:terms]:
        part = _dot(piece, b_bf16)
        acc = part if acc is None else acc + part
    return acc


def _dot_exact_lhs(a_bf16, b):
    b1, b2, b3 = _split3(b)
    return _dot(a_bf16, b1) + _dot(a_bf16, b2) + _dot(a_bf16, b3)


def _silu(x):
    return x * jax.nn.sigmoid(x)


def _tile_pos(t):
    is_ctx = t < NT_CTX
    j = jnp.maximum(t - NT_CTX, 0) % TILES_PER_LAT
    return is_ctx, j


def _lat_batch(t):
    return jnp.maximum(t - NT_CTX, 0) // TILES_PER_LAT


def _prev_halo(t):
    return jnp.maximum(t * (TILE // HALO) - 1, 0)


def _next_halo(t):
    return jnp.minimum((t + 1) * (TILE // HALO), N_HALO_BLOCKS - 1)


def _const_spec(shape):
    nd = len(shape)
    return pl.BlockSpec(shape, lambda *_: (0,) * nd, pipeline_mode=pl.Buffered(1))


def _layer_spec(shape, lidx):
    nd = len(shape)
    return pl.BlockSpec((1,) + tuple(shape), lambda *_: (lidx,) + (0,) * nd,
                        pipeline_mode=pl.Buffered(1))


def _row_spec(cols, fn=lambda i: i):
    return pl.BlockSpec((TILE, cols), lambda i: (fn(i), 0))


def _halo_spec(cols, which, fn=lambda i: i):
    pick = _prev_halo if which == "prev" else _next_halo
    return pl.BlockSpec((HALO, cols), lambda i: (pick(fn(i)), 0))


def _same_order(i):
    return i


def _lat_first_rows(i):
    return (i + NRT_CTX) % NRT


def _lat_first_tiles(i):
    return (i + NT_CTX) % NT


def _big_row_spec(cols, order=_same_order):
    return pl.BlockSpec((ROW_TILE, cols), lambda i: (order(i), 0))


def _big_halo_spec(cols, which, order=_same_order):
    per_tile = ROW_TILE // HALO
    if which == "prev":
        return pl.BlockSpec((HALO, cols), lambda i: (jnp.maximum(order(i) * per_tile - 1, 0), 0))
    return pl.BlockSpec((HALO, cols),
                        lambda i: (jnp.minimum((order(i) + 1) * per_tile, N_HALO_BLOCKS - 1), 0))


def _big_lat_index(t):
    rel = jnp.maximum(t - NRT_CTX, 0)
    return t < NRT_CTX, rel // ROW_TILES_PER_LAT, rel % ROW_TILES_PER_LAT


def _big_mod_spec(order=_same_order):
    def idx(i):
        is_ctx, b, _ = _big_lat_index(order(i))
        return (jnp.where(is_ctx, 0, 1 + b), 0, 0)
    return pl.BlockSpec((1, N_MOD, D_MODEL), idx)


def _params(n_axes=1):
    return pltpu.CompilerParams(dimension_semantics=("arbitrary",) * n_axes,
                                vmem_limit_bytes=VMEM_LIMIT)


def _rmsnorm_rows(x, g):
    ms = jnp.mean(x * x, axis=-1, keepdims=True)
    return x * lax.rsqrt(ms + EPS) * g


MOD_COLS = 1536


def _mod_kernel(c_ref, w_ref, b_ref, o_ref):
    s = _silu(c_ref[...]).astype(BF16)
    o_ref[0] = _dot(s, w_ref[0].astype(BF16)) + b_ref[0]


def _modulation(cvec, w_mod, b_mod):
    nmod = N_MOD * D_MODEL
    return pl.pallas_call(
        _mod_kernel,
        out_shape=jax.ShapeDtypeStruct((DEPTH, MOD_ROWS, nmod), F32),
        grid=(DEPTH, nmod // MOD_COLS),
        in_specs=[pl.BlockSpec((MOD_ROWS, D_MODEL), lambda l, j: (0, 0)),
                  pl.BlockSpec((1, D_MODEL, MOD_COLS), lambda l, j: (l, 0, j)),
                  pl.BlockSpec((1, 1, MOD_COLS), lambda l, j: (l, 0, j))],
        out_specs=pl.BlockSpec((1, MOD_ROWS, MOD_COLS), lambda l, j: (l, 0, j)),
        compiler_params=_params(2),
        name="modulation",
    )(cvec, w_mod, b_mod.reshape(DEPTH, 1, nmod))


_C_POOL = 0
_C_Q = _C_POOL + POOL_DIM
_C_K = _C_Q + ATT_QK_TOTAL
_C_V = _C_K + ATT_QK_TOTAL
_C_Z = _C_V + ATT_DIM
_C_XBC = _C_Z + SSD_DIM
_C_DT = _C_XBC + SSD_XBC_DIM
IN_DIM = _C_DT + 2 * SSD_HEADS
ROPE_SWAP = ATT_QK_DIM // 4
Q_PRESCALE = ATT_QK_DIM ** -0.5 * math.log2(math.e)


def _rope_block(xb, cos, sin):
    lane = lax.broadcasted_iota(jnp.int32, xb.shape, 1)
    first_half = (lane % (2 * ROPE_SWAP)) < ROPE_SWAP
    partner = jnp.where(first_half,
                        pltpu.roll(xb, LANE - ROPE_SWAP, axis=1),
                        pltpu.roll(xb, ROPE_SWAP, axis=1))
    return xb * cos + partner * sin


def _modulated_norm(x, g, mod_ref, shift_row):
    return (_rmsnorm_rows(x, g) * (1.0 + mod_ref[0, shift_row + 1:shift_row + 2, :])
            + mod_ref[0, shift_row:shift_row + 1, :])


N_INPROJ_IN = 9


def _inproj_kernel(*refs, has_cache_in):
    x_ref, xp_ref, xn_ref, mod_ref, g_ref, w_ref, wdt_ref, cos_ref, sin_ref = refs[:N_INPROJ_IN]
    outs = refs[N_INPROJ_IN + (2 if has_cache_in else 0):-1]
    pooled_ref, q_ref, kc_ref, vc_ref, kb_ref, vb_ref, z_ref, xbc_ref, dt_ref = outs
    pool_scr = refs[-1]
    is_ctx, _, j = _big_lat_index(_lat_first_rows(pl.program_id(0)))
    seqs = ROW_TILE // SEQ
    g = g_ref[0]
    hb = _modulated_norm(x_ref[...], g, mod_ref, 0).astype(BF16)
    w = w_ref.at[0]
    q = _dot(hb, w[:, _C_Q:_C_K])
    k = _dot(hb, w[:, _C_K:_C_V])
    cos = cos_ref[...]
    sin = sin_ref[...]
    for blk in range(ATT_QK_TOTAL // LANE):
        sl = slice(blk * LANE, (blk + 1) * LANE)
        q_ref[:, sl] = (_rope_block(q[:, sl], cos, sin) * Q_PRESCALE).astype(BF16)
        kr = _rope_block(k[:, sl], cos, sin)
        kc_ref[:, 0, :, sl] = kr.reshape(seqs, SEQ, LANE)
        kb_ref[:, sl] = kr.astype(BF16)
    h_halo = jnp.concatenate([_modulated_norm(xp_ref[...], g, mod_ref, 0),
                              _modulated_norm(xn_ref[...], g, mod_ref, 0)], axis=0).astype(BF16)
    u = _dot(jnp.concatenate([hb, h_halo], axis=0), w[:, _C_POOL:_C_Q])
    pooled_ref[...] = _pooled_deviation(u[0:ROW_TILE], u[ROW_TILE:ROW_TILE + HALO],
                                        u[ROW_TILE + HALO:], pool_scr, is_ctx, j)
    v = _dot(hb, w[:, _C_V:_C_Z])
    vc_ref[:, 0] = v.reshape(seqs, SEQ, ATT_DIM)
    vb_ref[...] = v.astype(BF16)
    z_ref[...] = _dot(hb, w[:, _C_Z:_C_XBC])
    dt_ref[...] = _dot(hb, wdt_ref[0])
    xbc_ref[...] = _dot(hb, w[:, _C_XBC:_C_DT])


def _inproj(x, mod_l, norm_g, w_in, w_dt, rope_cos, rope_sin, lidx, caches):
    order = _lat_first_rows
    seqs = ROW_TILE // SEQ
    cache_shape = (BATCH, DEPTH, SEQ, ATT_QK_TOTAL)
    row_outs = {0: (POOL_DIM, BF16), 1: (ATT_QK_TOTAL, BF16), 4: (ATT_QK_TOTAL, BF16), 5: (ATT_DIM, BF16),
                6: (SSD_DIM, F32), 7: (SSD_XBC_DIM, F32), 8: (DT_PAD, F32)}

    def cache_idx(i):
        t = order(i)
        return (jnp.where(t < NRT_CTX, t, 0), lidx, 0, 0)

    def rope_idx(i):
        is_ctx, _, j = _big_lat_index(order(i))
        return (jnp.where(is_ctx, 0, 1 + j), 0)

    out_shape, out_specs = [], []
    for n in range(9):
        if n in row_outs:
            c, d = row_outs[n]
            out_shape.append(jax.ShapeDtypeStruct((N_TOK, c), d))
            out_specs.append(_big_row_spec(c, order))
        else:
            out_shape.append(jax.ShapeDtypeStruct(cache_shape, F32))
            out_specs.append(pl.BlockSpec((seqs, 1, SEQ, ATT_QK_TOTAL), cache_idx))
    in_specs = [_big_row_spec(D_MODEL, order), _big_halo_spec(D_MODEL, "prev", order),
                _big_halo_spec(D_MODEL, "next", order), _big_mod_spec(order),
                _layer_spec((1, D_MODEL), lidx),
                _layer_spec((D_MODEL, IN_DIM), lidx),
                _layer_spec((D_MODEL, DT_PAD), lidx),
                pl.BlockSpec((ROW_TILE, LANE), rope_idx),
                pl.BlockSpec((ROW_TILE, LANE), rope_idx)]
    args = [x, x, x, mod_l, norm_g, w_in, w_dt, rope_cos, rope_sin]
    assert len(args) == N_INPROJ_IN
    aliases = {}
    if caches is not None:
        in_specs += [pl.BlockSpec(memory_space=pl.ANY)] * 2
        args += list(caches)
        aliases = {N_INPROJ_IN: 2, N_INPROJ_IN + 1: 3}
    return pl.pallas_call(
        functools.partial(_inproj_kernel, has_cache_in=caches is not None),
        out_shape=out_shape,
        grid=(NRT,),
        in_specs=in_specs,
        out_specs=out_specs,
        scratch_shapes=[pltpu.VMEM((POOL_SCR_ROWS, POOL_DIM), F32)],
        input_output_aliases=aliases,
        compiler_params=_params(),
        name="inproj",
    )(*args)


def _fill_halo_scratch(scr, prev_ref, cur, next_ref, first, last):
    scr[0:HALO, :] = jnp.where(first, 0.0, prev_ref[...])
    scr[HALO:HALO + TILE, :] = cur
    scr[HALO + TILE:, :] = jnp.where(last, 0.0, next_ref[...])


POOL_HALF_STRIDE = SEQ + 2 * HALO
POOL_SCR_ROWS = (ROW_TILE // SEQ) * POOL_HALF_STRIDE


def _pooled_deviation(u, u_prev, u_next, scr, is_ctx, j):
    n_half = ROW_TILE // SEQ
    first = jnp.logical_or(is_ctx, j == 0)
    last = jnp.logical_or(is_ctx, j == ROW_TILES_PER_LAT - 1)
    seq_len = jnp.where(is_ctx, SEQ, DEC_SEQ)
    for hf in range(n_half):
        base = hf * POOL_HALF_STRIDE
        if hf == 0:
            before = jnp.where(first, 0.0, u_prev)
        else:
            before = jnp.where(is_ctx, 0.0, u[hf * SEQ - HALO:hf * SEQ, :])
        if hf == n_half - 1:
            after = jnp.where(last, 0.0, u_next)
        else:
            after = jnp.where(is_ctx, 0.0, u[(hf + 1) * SEQ:(hf + 1) * SEQ + HALO, :])
        scr[base:base + HALO, :] = before
        scr[base + HALO:base + HALO + SEQ, :] = u[hf * SEQ:(hf + 1) * SEQ, :]
        scr[base + HALO + SEQ:base + POOL_HALF_STRIDE, :] = after
    halves = []
    for hf in range(n_half):
        base = hf * POOL_HALF_STRIDE + HALO
        pos = (jnp.where(is_ctx, 0, j * ROW_TILE + hf * SEQ)
               + lax.broadcasted_iota(jnp.int32, (SEQ, 1), 0))
        groups = []
        for g, win in enumerate(POOL_WINDOWS):
            half = win // 2
            sl = slice(g * POOL_GROUP_DIM, (g + 1) * POOL_GROUP_DIM)
            acc = scr[base - half:base - half + SEQ, sl]
            for k in range(1 - half, half):
                acc = acc + scr[base + k:base + k + SEQ, sl]
            lo = jnp.maximum(pos - half, 0)
            hi = jnp.minimum(pos + half - 1, seq_len - 1)
            inv_count = 1.0 / (hi - lo + 1).astype(F32)
            groups.append((acc * inv_count - scr[base:base + SEQ, sl]).astype(BF16))
        halves.append(jnp.concatenate(groups, axis=1))
    return jnp.concatenate(halves, axis=0)


def _attn_kernel(q_ref, kc_ref, vc_ref, kl_ref, vl_ref, kp_ref, vp_ref, lam_ref, g_ref, o_ref,
                 s_scr, *, lam_init):
    i = pl.program_id(0)
    lp = lam_ref[0]
    lam = (jnp.exp(jnp.sum(lp[0:1] * lp[1:2], axis=-1, keepdims=True))
           - jnp.exp(jnp.sum(lp[2:3] * lp[3:4], axis=-1, keepdims=True)) + lam_init)

    def run(key_blocks):
        for h in range(ATT_HEADS):
            sl = slice(h * LANE, (h + 1) * LANE)
            qh = q_ref[:, sl]
            lane = lax.broadcasted_iota(jnp.int32, qh.shape, 1)
            zero = jnp.zeros_like(qh)
            q2 = jnp.concatenate([jnp.where(lane < ATT_QK_DIM, qh, zero),
                                  jnp.where(lane >= ATT_QK_DIM, qh, zero)], axis=0)
            run_max = None
            for b, (kr, _, r0) in enumerate(key_blocks):
                s = _dot_nt(q2, kr[r0:r0 + KEY_BLOCK, sl])
                s_scr[:, b * KEY_BLOCK:(b + 1) * KEY_BLOCK] = s
                blk_max = jnp.maximum(s[:, 0:LANE], s[:, LANE:KEY_BLOCK])
                run_max = blk_max if run_max is None else jnp.maximum(run_max, blk_max)
            m = jnp.max(run_max, axis=-1, keepdims=True)
            run_sum = None
            acc = None
            for b, (_, vr, r0) in enumerate(key_blocks):
                p = jnp.exp2(s_scr[:, b * KEY_BLOCK:(b + 1) * KEY_BLOCK] - m)
                blk_sum = p[:, 0:LANE] + p[:, LANE:KEY_BLOCK]
                run_sum = blk_sum if run_sum is None else run_sum + blk_sum
                part = _dot(p.astype(BF16), vr[r0:r0 + KEY_BLOCK, sl])
                acc = part if acc is None else acc + part
            o2 = acc / jnp.sum(run_sum, axis=-1, keepdims=True)
            oh = o2[0:TILE] - lam * o2[TILE:]
            oh = _rmsnorm_rows(oh, g_ref[0]) * (1.0 - lam_init)
            o_ref[:, sl] = oh.astype(BF16)

    @pl.when(i < NT_CTX)
    def _():
        run([(kc_ref, vc_ref, 0)])

    @pl.when(i >= NT_CTX)
    def _():
        run([(kp_ref.at[0, 0], vp_ref.at[0, 0], 0)]
            + [(kl_ref, vl_ref, r0) for r0 in range(0, DEC_SEQ, KEY_BLOCK)])


def _diff_attention(q, kb, vb, past_k, past_v, att_lambda, subln_g, lidx):
    lam_init = 0.8 - 0.6 * math.exp(-0.3 * lidx)
    ctx_idx = lambda i: (jnp.minimum(i, NT_CTX - 1), 0)
    lat_idx = lambda i: (N_CTX // DEC_SEQ + _lat_batch(i), 0)
    past_idx = lambda i: (_lat_batch(i), lidx, 0, 0)
    return pl.pallas_call(
        functools.partial(_attn_kernel, lam_init=lam_init),
        out_shape=jax.ShapeDtypeStruct((N_TOK, ATT_DIM), BF16),
        grid=(NT,),
        in_specs=[_row_spec(ATT_QK_TOTAL),
                  pl.BlockSpec((TILE, ATT_QK_TOTAL), ctx_idx),
                  pl.BlockSpec((TILE, ATT_DIM), ctx_idx),
                  pl.BlockSpec((DEC_SEQ, ATT_QK_TOTAL), lat_idx),
                  pl.BlockSpec((DEC_SEQ, ATT_DIM), lat_idx),
                  pl.BlockSpec((1, 1, PAST_LEN, ATT_QK_TOTAL), past_idx),
                  pl.BlockSpec((1, 1, PAST_LEN, ATT_DIM), past_idx),
                  _layer_spec((4, ATT_QK_DIM), lidx),
                  _layer_spec((1, ATT_V_DIM), lidx)],
        out_specs=_row_spec(ATT_DIM),
        scratch_shapes=[pltpu.VMEM((2 * TILE, N_KEYS_LAT), F32)],
        compiler_params=_params(),
        name="diff_attention",
    )(q, kb, vb, kb, vb, past_k, past_v, att_lambda, subln_g)


def _ssd_scan_tile(direction, t, xc_scr, dt_ref, h0_ref, dtb_ref, asmall_ref, e_ref, st_ref,
                   state_scr, y_scr):
    is_ctx, j = _tile_pos(t)
    seq_begins = (j == 0) if direction == 0 else (j == TILES_PER_LAT - 1)

    @pl.when(is_ctx)
    def _():
        state_scr[...] = jnp.zeros_like(state_scr)

    @pl.when(jnp.logical_and(jnp.logical_not(is_ctx), seq_begins))
    def _():
        state_scr[...] = h0_ref[0, 0, 0].T

    a_small = -jnp.exp(asmall_ref[...])
    row = lax.broadcasted_iota(jnp.int32, (SSD_CHUNK, SSD_CHUNK), 0)
    col = lax.broadcasted_iota(jnp.int32, (SSD_CHUNK, SSD_CHUNK), 1)
    keep = (col <= row) if direction == 0 else (col >= row)
    tri = jnp.where(keep, 1.0, 0.0).astype(BF16)
    pair_lane = lax.broadcasted_iota(jnp.int32, (SSD_CHUNK, LANE), 1)
    edge = SSD_CHUNK - 1 if direction == 0 else 0
    heads_per_group = SSD_HEADS // SSD_GROUPS

    chunks = (0, 1) if direction == 0 else (1, 0)
    for c in chunks:
        rows = slice(c * SSD_CHUNK, (c + 1) * SSD_CHUNK)
        dt = jax.nn.softplus(dt_ref[rows, :] + dtb_ref[...])
        dt_exp = _dot_split(dt, e_ref[...], 2)
        acs_s = _dot_exact_lhs(tri, dt * a_small)
        acs_exp = _dot_split(acs_s, e_ref[...], 3)
        acs_st = acs_s.T
        xdt = xc_scr[rows, 0:SSD_DIM] * dt_exp
        total = acs_exp[edge:edge + 1, :]
        xw = (xdt * jnp.exp(total - acs_exp)).astype(BF16)
        decay_from_entry = jnp.exp(acs_exp)
        for g in range(SSD_GROUPS):
            gs = slice(g * SSD_GROUP_DIM, (g + 1) * SSD_GROUP_DIM)
            b_g = xc_scr[rows, SSD_DIM + g * SSD_STATE:SSD_DIM + (g + 1) * SSD_STATE]
            c_g = xc_scr[rows, SSD_DIM + (SSD_GROUPS + g) * SSD_STATE:
                         SSD_DIM + (SSD_GROUPS + g + 1) * SSD_STATE].astype(BF16)
            cb = _dot_nt(c_g, b_g.astype(BF16))
            h_in = state_scr[:, gs]
            y_g = _dot(c_g, h_in.astype(BF16)) * decay_from_entry[:, gs]
            pairs = []
            for pr in range(heads_per_group // 2):
                ps = slice(g * SSD_GROUP_DIM + pr * LANE, g * SSD_GROUP_DIM + (pr + 1) * LANE)
                x_pair = xdt[:, ps].astype(BF16)
                halves = []
                for hh in range(2):
                    hcol = direction * SSD_HEADS + g * heads_per_group + pr * 2 + hh
                    diff = acs_s[:, hcol:hcol + 1] - acs_st[hcol:hcol + 1, :]
                    decay = jnp.where(keep, jnp.exp(diff), 0.0)
                    halves.append(_dot((cb * decay).astype(BF16), x_pair))
                pairs.append(jnp.where(pair_lane < SSD_HEAD_DIM, halves[0], halves[1]))
            y_g = y_g + jnp.concatenate(pairs, axis=1)
            y_scr[rows, gs] = y_g
            state_scr[:, gs] = (h_in * jnp.exp(total[:, gs])
                                + _dot(b_g.T.astype(BF16), xw[:, gs]))
    st_ref[0, 0, 0] = state_scr[...].T


N_SSD_FWD_IN = 10


def _ssd_fwd_kernel(*refs, has_state_in):
    (xbc_ref, xp_ref, xn_ref, dt_ref, h0_ref, cw_ref, cb_ref, dtb_ref, asmall_ref,
     e_ref) = refs[:N_SSD_FWD_IN]
    y_ref, xc_ref, st_ref, conv_scr, state_scr = refs[N_SSD_FWD_IN + (1 if has_state_in else 0):]
    t = _lat_first_tiles(pl.program_id(0))
    is_ctx, j = _tile_pos(t)
    first = jnp.logical_or(is_ctx, j == 0)
    last = jnp.logical_or(is_ctx, j == TILES_PER_LAT - 1)
    _fill_halo_scratch(conv_scr, xp_ref, xbc_ref[...], xn_ref, first, last)
    conv = (cw_ref[0, 0:1, :] * conv_scr[HALO - 1:HALO - 1 + TILE, :]
            + cw_ref[0, 1:2, :] * conv_scr[HALO:HALO + TILE, :]
            + cw_ref[0, 2:3, :] * conv_scr[HALO + 1:HALO + 1 + TILE, :] + cb_ref[0])
    xc_ref[...] = _silu(conv)
    _ssd_scan_tile(0, t, xc_ref, dt_ref, h0_ref, dtb_ref, asmall_ref, e_ref, st_ref, state_scr, y_ref)


def _ssd_bwd_kernel(xc_ref, dt_ref, h0_ref, dtb_ref, asmall_ref, e_ref, yf_ref, z_ref, dskip_ref, ng_ref,
                    st_in_ref, o_ref, st_ref, state_scr, y_scr):
    del st_in_ref
    t = NT - 1 - pl.program_id(0)
    _ssd_scan_tile(1, t, xc_ref, dt_ref, h0_ref, dtb_ref, asmall_ref, e_ref, st_ref, state_scr, y_scr)
    y = yf_ref[...] + y_scr[...] + dskip_ref[...] * xc_ref[:, 0:SSD_DIM]
    yz = y * _silu(z_ref[...])
    for g in range(SSD_GROUPS):
        gs = slice(g * SSD_GROUP_DIM, (g + 1) * SSD_GROUP_DIM)
        o_ref[:, gs] = _rmsnorm_rows(yz[:, gs], ng_ref[0, :, gs]).astype(BF16)


def _ssd_mixer(xbc, dt_raw, z, state_ssd, lidx, conv_w, conv_b, norm_g, p, new_state):
    st_shape = jax.ShapeDtypeStruct((BATCH, DEPTH, 2, SSD_DIM, SSD_STATE), F32)
    st_block = (1, 1, 1, SSD_DIM, SSD_STATE)
    rev = lambda i: NT - 1 - i
    fwd = _lat_first_tiles
    state_scratch = pltpu.VMEM((SSD_STATE, SSD_DIM), F32)
    any_spec = pl.BlockSpec(memory_space=pl.ANY)

    def st_idx(order, direction, parked):
        def idx(i):
            t = order(i)
            return (jnp.where(t < NT_CTX, t, parked), lidx, direction, 0, 0)
        return idx

    def scan_specs(order, direction):
        h0_idx = lambda i: (_lat_batch(order(i)), lidx, direction, 0, 0)
        return [_row_spec(DT_PAD, order),
                pl.BlockSpec((1, 1, 1, SSD_DIM, SSD_STATE), h0_idx),
                _const_spec((1, DT_PAD)), _const_spec((1, DT_PAD)), _const_spec((DT_PAD, SSD_DIM))]

    def scan_args(direction):
        return [dt_raw, state_ssd, p["dt_bias"], p["alog_small"][direction], p["expand"][direction]]

    fwd_specs = ([_row_spec(SSD_XBC_DIM, fwd), _halo_spec(SSD_XBC_DIM, "prev", fwd),
                  _halo_spec(SSD_XBC_DIM, "next", fwd)]
                 + scan_specs(fwd, 0)[:2]
                 + [_layer_spec((3, SSD_XBC_DIM), lidx), _layer_spec((1, SSD_XBC_DIM), lidx)]
                 + scan_specs(fwd, 0)[2:])
    fwd_args = [xbc, xbc, xbc, *scan_args(0)[:2], conv_w, conv_b, *scan_args(0)[2:]]
    assert len(fwd_args) == N_SSD_FWD_IN
    fwd_alias = {}
    if new_state is not None:
        fwd_specs.append(any_spec)
        fwd_args.append(new_state)
        fwd_alias = {N_SSD_FWD_IN: 2}
    y_f, xc, new_state = pl.pallas_call(
        functools.partial(_ssd_fwd_kernel, has_state_in=bool(fwd_alias)),
        out_shape=[jax.ShapeDtypeStruct((N_TOK, SSD_DIM), F32),
                   jax.ShapeDtypeStruct((N_TOK, SSD_XBC_DIM), F32), st_shape],
        grid=(NT,),
        in_specs=fwd_specs,
        out_specs=[_row_spec(SSD_DIM, fwd), _row_spec(SSD_XBC_DIM, fwd),
                   pl.BlockSpec(st_block, st_idx(fwd, 0, 0))],
        scratch_shapes=[pltpu.VMEM((TILE + 2 * HALO, SSD_XBC_DIM), F32), state_scratch],
        input_output_aliases=fwd_alias,
        compiler_params=_params(),
        name="ssd_forward",
    )(*fwd_args)

    bwd_args = [xc, *scan_args(1), y_f, z, p["d_skip"], norm_g, new_state]
    out, new_state = pl.pallas_call(
        _ssd_bwd_kernel,
        out_shape=[jax.ShapeDtypeStruct((N_TOK, SSD_DIM), BF16), st_shape],
        grid=(NT,),
        in_specs=([_row_spec(SSD_XBC_DIM, rev)] + scan_specs(rev, 1)
                  + [_row_spec(SSD_DIM, rev), _row_spec(SSD_DIM, rev),
                     _const_spec((1, SSD_DIM)), _layer_spec((1, SSD_DIM), lidx), any_spec]),
        out_specs=[_row_spec(SSD_DIM, rev), pl.BlockSpec(st_block, st_idx(rev, 1, NT_CTX - 1))],
        scratch_shapes=[state_scratch, pltpu.VMEM((TILE, SSD_DIM), F32)],
        input_output_aliases={len(bwd_args) - 1: 1},
        compiler_params=_params(),
        name="ssd_backward",
    )(*bwd_args)
    return out, new_state


def _merge_kernel(x_ref, mod_ref, g_ref, wg_ref, bg_ref, pooled_ref, pw_ref, ps_ref,
                  oa_ref, os_ref, wb_ref, wo_ref, o_ref):
    x = x_ref[...]
    hb = _modulated_norm(x, g_ref[0], mod_ref, 0).astype(BF16)
    wg = wg_ref.at[0]
    wb = wb_ref.at[0]

    def gated(n, branch, row0, width):
        cols = slice(n * D_MODEL, (n + 1) * D_MODEL)
        gate = jax.nn.sigmoid(_dot(hb, wg[:, cols]) + bg_ref[0, :, cols])
        return gate * _dot(branch, wb[row0:row0 + width, :])

    o_pool = jnp.concatenate(
        [_dot(pooled_ref[:, g * POOL_GROUP_DIM:(g + 1) * POOL_GROUP_DIM], pw_ref[0, g])
         for g in range(POOL_GROUPS)], axis=1) * ps_ref[0]
    merged = (gated(0, o_pool.astype(BF16), 0, POOL_DIM)
              + gated(1, oa_ref[...], POOL_DIM, ATT_DIM)
              + gated(2, os_ref[...], POOL_DIM + ATT_DIM, SSD_DIM))
    o_ref[...] = x + mod_ref[0, 2:3, :] * _dot(merged.astype(BF16), wo_ref[0])


def _merge(x, mod_l, norm_g, w_gate, b_gate, pooled, pool_w, pool_scale, o_att, o_ssd, w_branch, w_out,
           lidx):
    return pl.pallas_call(
        _merge_kernel,
        out_shape=jax.ShapeDtypeStruct((N_TOK, D_MODEL), F32),
        grid=(NRT,),
        in_specs=[_big_row_spec(D_MODEL), _big_mod_spec(),
                  _layer_spec((1, D_MODEL), lidx),
                  _layer_spec((D_MODEL, 3 * D_MODEL), lidx), _layer_spec((1, 3 * D_MODEL), lidx),
                  _big_row_spec(POOL_DIM),
                  _layer_spec((POOL_GROUPS, POOL_GROUP_DIM, POOL_GROUP_DIM), lidx),
                  _layer_spec((1, POOL_DIM), lidx),
                  _big_row_spec(ATT_DIM), _big_row_spec(SSD_DIM),
                  _layer_spec((MIX_DIM, D_MODEL), lidx), _layer_spec((D_MODEL, D_MODEL), lidx)],
        out_specs=_big_row_spec(D_MODEL),
        compiler_params=_params(),
        name="merge",
    )(x, mod_l, norm_g, w_gate, b_gate, pooled, pool_w, pool_scale, o_att, o_ssd, w_branch, w_out)


N_FFN_IN = 10


def _ffn_kernel(*refs, final):
    x_ref, xp_ref, xn_ref, mod_ref, g_ref, wu_ref, cw_ref, cb_ref, wd_ref, fg_ref = refs[:N_FFN_IN]
    h_scr, cols_scr = refs[-2:]
    out_refs = refs[N_FFN_IN:-2]
    res_ref = h_scr if final else out_refs[0]
    i = pl.program_id(0)
    is_ctx, _, j = _big_lat_index(i)
    first = jnp.logical_or(is_ctx, j == 0)
    last = jnp.logical_or(is_ctx, j == ROW_TILES_PER_LAT - 1)
    seq_len = jnp.where(is_ctx, SEQ, DEC_SEQ)
    tile_pos = jnp.where(is_ctx, 0, j * ROW_TILE)
    n_col = D_MODEL // LANE

    g = g_ref[0]
    h_nat = _modulated_norm(x_ref[...], g, mod_ref, 3)
    for cb in range(n_col):
        cols_scr[cb] = h_nat[:, cb * LANE:(cb + 1) * LANE]
    for r in range(ROW_GROUPS):
        for cb in range(n_col):
            h_scr[r * SUBLANE:(r + 1) * SUBLANE, cb * LANE:(cb + 1) * LANE] = (
                cols_scr[cb, pl.ds(r, SUBLANE, stride=ROW_GROUPS), :])
    h_scr[ROW_TILE:ROW_TILE + HALO, :] = jnp.where(first, 0.0, _modulated_norm(xp_ref[...], g, mod_ref, 3))
    h_scr[ROW_TILE + HALO:, :] = jnp.where(last, 0.0, _modulated_norm(xn_ref[...], g, mod_ref, 3))
    hb = h_scr[...].astype(BF16)

    sub = lax.broadcasted_iota(jnp.int32, (SUBLANE, 1), 0)
    pos_first = tile_pos + sub * ROW_GROUPS
    pos_last = pos_first + (ROW_GROUPS - 1)
    at_seq_start = (pos_first % seq_len) == 0
    at_seq_end = (pos_last % seq_len) == seq_len - 1
    wu = wu_ref.at[0]

    def conv_half(col0):
        cols = slice(col0, col0 + FFN_CHUNK)
        up = _dot(hb, wu[:, cols])
        a = up[0:ROW_TILE]
        halo_prev = up[ROW_TILE + HALO - 1:ROW_TILE + HALO]
        halo_next = up[ROW_TILE + HALO:ROW_TILE + HALO + 1]
        wrap_prev = pltpu.roll(a[ROW_TILE - SUBLANE:ROW_TILE], 1, axis=0)
        wrap_prev = jnp.where(sub == 0, halo_prev, wrap_prev)
        wrap_prev = jnp.where(at_seq_start, 0.0, wrap_prev)
        wrap_next = pltpu.roll(a[0:SUBLANE], SUBLANE - 1, axis=0)
        wrap_next = jnp.where(sub == SUBLANE - 1, halo_next, wrap_next)
        wrap_next = jnp.where(at_seq_end, 0.0, wrap_next)
        before = jnp.concatenate([wrap_prev, a[0:ROW_TILE - SUBLANE]], axis=0)
        after = jnp.concatenate([a[SUBLANE:ROW_TILE], wrap_next], axis=0)
        return (cw_ref[0, 0:1, cols] * before + cw_ref[0, 1:2, cols] * a
                + cw_ref[0, 2:3, cols] * after + cb_ref[0, :, cols])

    acc = None
    for c in range(FFN_DIM // FFN_CHUNK):
        a_gate = conv_half(c * FFN_CHUNK)
        a_val = conv_half(FFN_DIM + c * FFN_CHUNK)
        act = (_silu(a_gate) * a_val).astype(BF16)
        part = _dot(act, wd_ref[0, c * FFN_CHUNK:(c + 1) * FFN_CHUNK, :])
        acc = part if acc is None else acc + part
    upd = mod_ref[0, 5:6, :] * acc
    for cb in range(n_col):
        cols_scr[cb] = upd[:, cb * LANE:(cb + 1) * LANE]
    groups_per_sub = ROW_GROUPS // SUBLANE
    for q in range(ROW_GROUPS):
        start = (q % groups_per_sub) * ROW_GROUPS + q // groups_per_sub
        rows = slice(q * SUBLANE, (q + 1) * SUBLANE)
        for cb in range(n_col):
            cs = slice(cb * LANE, (cb + 1) * LANE)
            res_ref[rows, cs] = x_ref[rows, cs] + cols_scr[cb, pl.ds(start, SUBLANE, stride=SUBLANE), :]
    if final:
        y = _rmsnorm_rows(res_ref[0:ROW_TILE, :], fg_ref[...])
        y_ctx_ref, y_lat_ref = out_refs

        @pl.when(is_ctx)
        def _():
            y_ctx_ref[...] = y

        @pl.when(jnp.logical_not(is_ctx))
        def _():
            y_lat_ref[...] = y


def _conv_ffn(x, mod_l, norm_g, w_up, conv_w, conv_b, w_down, final_g, lidx, final):
    if final:
        out_shape = [jax.ShapeDtypeStruct((N_CTX, D_MODEL), F32), jax.ShapeDtypeStruct((N_LAT, D_MODEL), F32)]
        out_specs = [pl.BlockSpec((ROW_TILE, D_MODEL), lambda i: (jnp.minimum(i, NRT_CTX - 1), 0)),
                     pl.BlockSpec((ROW_TILE, D_MODEL), lambda i: (jnp.maximum(i - NRT_CTX, 0), 0))]
    else:
        out_shape = jax.ShapeDtypeStruct((N_TOK, D_MODEL), F32)
        out_specs = _big_row_spec(D_MODEL)
    return pl.pallas_call(
        functools.partial(_ffn_kernel, final=final),
        out_shape=out_shape,
        grid=(NRT,),
        in_specs=[_big_row_spec(D_MODEL), _big_halo_spec(D_MODEL, "prev"), _big_halo_spec(D_MODEL, "next"),
                  _big_mod_spec(),
                  _layer_spec((1, D_MODEL), lidx),
                  _layer_spec((D_MODEL, 2 * FFN_DIM), lidx), _layer_spec((3, 2 * FFN_DIM), lidx),
                  _layer_spec((1, 2 * FFN_DIM), lidx), _layer_spec((FFN_DIM, D_MODEL), lidx),
                  _const_spec((1, D_MODEL))],
        out_specs=out_specs,
        scratch_shapes=[pltpu.VMEM((ROW_TILE + 2 * HALO, D_MODEL), F32),
                        pltpu.VMEM((D_MODEL // LANE, ROW_TILE, LANE), F32)],
        compiler_params=_params(),
        name="conv_ffn",
    )(x, x, x, mod_l, norm_g, w_up, conv_w, conv_b, w_down, final_g)


def _rope_tables():
    rows = DEC_SEQ // GRID_W
    row = jnp.repeat(jnp.arange(rows), GRID_W).astype(F32)
    col = jnp.tile(jnp.arange(GRID_W), rows).astype(F32)
    half = ATT_QK_DIM // 2
    inv = ROPE_BASE ** (-jnp.arange(0, half, 2, dtype=F32) / half)
    ar, ac = row[:, None] * inv, col[:, None] * inv
    cos = jnp.concatenate([jnp.cos(ar), jnp.cos(ar), jnp.cos(ac), jnp.cos(ac)], axis=-1)
    sin = jnp.concatenate([-jnp.sin(ar), jnp.sin(ar), -jnp.sin(ac), jnp.sin(ac)], axis=-1)
    reps = LANE // ATT_QK_DIM
    cos = jnp.concatenate([jnp.ones((ROW_TILE, LANE), F32), jnp.tile(cos, (1, reps))], axis=0)
    sin = jnp.concatenate([jnp.zeros((ROW_TILE, LANE), F32), jnp.tile(sin, (1, reps))], axis=0)
    return cos, sin


def _expand_matrices():
    e = np.zeros((2, DT_PAD, SSD_DIM), np.float32)
    for d in range(2):
        for h in range(SSD_HEADS):
            e[d, d * SSD_HEADS + h, h * SSD_HEAD_DIM:(h + 1) * SSD_HEAD_DIM] = 1.0
    return jnp.asarray(e, BF16)


def _ssd_small_params(l, ssd_dt_bias, ssd_A_log, ssd_D):
    n_dt = 2 * SSD_HEADS
    alog = ssd_A_log[l]
    alog_small = jnp.zeros((2, 1, DT_PAD), F32)
    for d in range(2):
        alog_small = alog_small.at[d, 0, d * SSD_HEADS:(d + 1) * SSD_HEADS].set(alog[d])
    return dict(
        dt_bias=jnp.pad(ssd_dt_bias[l].reshape(1, n_dt), ((0, 0), (0, DT_PAD - n_dt))),
        alog_small=alog_small,
        expand=_expand_matrices(),
        d_skip=jnp.repeat(ssd_D[l], SSD_HEAD_DIM)[None],
    )


def kernel(x_prompt, x_sample, c, cache_attn_k, cache_attn_v, state_ssd, c_ctx, w_mod, b_mod, norm1_g, w_in, pool_w, pool_scale, att_lambda, att_subln_g, ssd_conv_w, ssd_conv_b, ssd_dt_bias, ssd_A_log, ssd_D, ssd_norm_g, w_gate, b_gate, w_branch, w_out, norm2_g, ffn_w_up, ffn_conv_w, ffn_conv_b, ffn_w_down, final_norm_g):
    x = jnp.concatenate([x_prompt.reshape(N_CTX, D_MODEL), x_sample.reshape(N_LAT, D_MODEL)], axis=0)
    cvec = jnp.concatenate([c_ctx[None], c, jnp.zeros((MOD_ROWS - 1 - DEC_BATCH, D_MODEL), F32)], axis=0)
    mod = _modulation(cvec, w_mod, b_mod).reshape(DEPTH, MOD_ROWS, N_MOD, D_MODEL)
    rope_cos, rope_sin = _rope_tables()
    past_k = cache_attn_k.reshape(DEC_BATCH, DEPTH, PAST_LEN, ATT_QK_TOTAL).astype(BF16)
    past_v = cache_attn_v.reshape(DEC_BATCH, DEPTH, PAST_LEN, ATT_DIM).astype(BF16)
    past_state = state_ssd.reshape(DEC_BATCH, DEPTH, 2, SSD_DIM, SSD_STATE)

    n_dt = 2 * SSD_HEADS
    w_in_b = w_in.astype(BF16)
    w_dt_b = jnp.pad(w_in[:, :, _C_DT:], ((0, 0), (0, 0), (0, DT_PAD - n_dt))).astype(BF16)
    w_gate_b, w_branch_b, w_out_b = w_gate.astype(BF16), w_branch.astype(BF16), w_out.astype(BF16)
    w_up_b, w_down_b, pool_w_b = ffn_w_up.astype(BF16), ffn_w_down.astype(BF16), pool_w.astype(BF16)
    row = lambda a: a[:, None, :]
    norm1_r, norm2_r, b_gate_r, pool_scale_r = row(norm1_g), row(norm2_g), row(b_gate), row(pool_scale)
    subln_r, conv_b_r, ssd_norm_r, ffn_conv_b_r = row(att_subln_g), row(ssd_conv_b), row(ssd_norm_g), row(ffn_conv_b)
    final_g = final_norm_g[None]

    caches, new_state = None, None
    for l in range(DEPTH):
        pooled, q, k_cache, v_cache, kb, vb, z, xbc, dt_raw = _inproj(x, mod[l], norm1_r, w_in_b, w_dt_b,
                                                                      rope_cos, rope_sin, l, caches)
        caches = (k_cache, v_cache)
        o_att = _diff_attention(q, kb, vb, past_k, past_v, att_lambda, subln_r, l)
        o_ssd, new_state = _ssd_mixer(xbc, dt_raw, z, past_state, l, ssd_conv_w, conv_b_r, ssd_norm_r,
                                      _ssd_small_params(l, ssd_dt_bias, ssd_A_log, ssd_D), new_state)
        x = _merge(x, mod[l], norm1_r, w_gate_b, b_gate_r, pooled, pool_w_b, pool_scale_r, o_att, o_ssd,
                   w_branch_b, w_out_b, l)
        x = _conv_ffn(x, mod[l], norm2_r, w_up_b, ffn_conv_w, ffn_conv_b_r, w_down_b, final_g, l,
                      final=(l == DEPTH - 1))

    y_ctx, y_lat = x
    return (y_ctx.reshape(BATCH, SEQ, D_MODEL), y_lat.reshape(DEC_BATCH, DEC_SEQ, D_MODEL),
            caches[0].reshape(BATCH, DEPTH, SEQ, ATT_HEADS, 2, ATT_QK_DIM),
            caches[1].reshape(BATCH, DEPTH, SEQ, ATT_HEADS, ATT_V_DIM),
            new_state.reshape(BATCH, DEPTH, 2, SSD_HEADS, SSD_HEAD_DIM, SSD_STATE))
```

```python
import functools
import math

import numpy as np
import jax
import jax.numpy as jnp
from jax import lax
from jax.experimental import pallas as pl
from jax.experimental.pallas import tpu as pltpu

D_MODEL = 1024
BATCH = 32
SEQ = 256
DEPTH = 4
DEC_BATCH = 4
DEC_SEQ = 2048
PAST_LEN = 256
GRID_W = 64
EPS = 1e-6
N_MOD = 6

POOL_GROUPS = 4
POOL_WINDOWS = (2, 4, 8, 16)
POOL_DIM = D_MODEL // 2
POOL_GROUP_DIM = POOL_DIM // POOL_GROUPS

ATT_HEADS = 4
ATT_QK_DIM = 64
ATT_V_DIM = 2 * ATT_QK_DIM
ATT_QK_TOTAL = ATT_HEADS * 2 * ATT_QK_DIM
ATT_DIM = ATT_HEADS * ATT_V_DIM
ROPE_BASE = 10000.0

SSD_DIM = D_MODEL // 2
SSD_HEAD_DIM = 64
SSD_HEADS = SSD_DIM // SSD_HEAD_DIM
SSD_GROUPS = 2
SSD_STATE = 128
SSD_CHUNK = 128
SSD_XBC_DIM = SSD_DIM + 2 * SSD_GROUPS * SSD_STATE
SSD_GROUP_DIM = SSD_DIM // SSD_GROUPS

MIX_DIM = POOL_DIM + ATT_DIM + SSD_DIM
FFN_DIM = ((8 * D_MODEL // 3 + 127) // 128) * 128

LANE = 128
SUBLANE = 8
HALO = SUBLANE

TILE = 256
N_CTX = BATCH * SEQ
N_LAT = DEC_BATCH * DEC_SEQ
N_TOK = N_CTX + N_LAT
NT_CTX = N_CTX // TILE
NT = N_TOK // TILE
TILES_PER_LAT = DEC_SEQ // TILE
N_HALO_BLOCKS = N_TOK // HALO
MOD_ROWS = 8
DT_PAD = LANE
FFN_CHUNK = FFN_DIM // 2
VMEM_LIMIT = 56 * 1024 * 1024

ROW_TILE = 512
NRT = N_TOK // ROW_TILE
NRT_CTX = N_CTX // ROW_TILE
ROW_TILES_PER_LAT = DEC_SEQ // ROW_TILE
ROW_GROUPS = ROW_TILE // SUBLANE

KEY_BLOCK = 256
N_KEYS_LAT = PAST_LEN + DEC_SEQ

F32 = jnp.float32
BF16 = jnp.bfloat16

assert SEQ == TILE and DEC_SEQ % TILE == 0 and PAST_LEN == TILE
assert TILE == 2 * SSD_CHUNK and SSD_STATE == LANE and FFN_CHUNK % LANE == 0
assert ROW_TILE % SEQ == 0 and DEC_SEQ % ROW_TILE == 0
assert KEY_BLOCK == PAST_LEN == SEQ and DEC_SEQ % KEY_BLOCK == 0 and KEY_BLOCK == 2 * LANE


def _dot(a, b):
    return jnp.dot(a, b, preferred_element_type=F32)


def _dot_nt(a, b):
    return lax.dot_general(a, b, (((1,), (1,)), ((), ())), preferred_element_type=F32)


def _split3(x):
    x1 = x.astype(BF16)
    r1 = x - x1.astype(F32)
    x2 = r1.astype(BF16)
    x3 = (r1 - x2.astype(F32)).astype(BF16)
    return x1, x2, x3


def _dot_split(a, b_bf16, terms):
    acc = None
    for piece in _split3(a)[:terms]:
        part = _dot(piece, b_bf16)
        acc = part if acc is None else acc + part
    return acc


def _dot_exact_lhs(a_bf16, b):
    b1, b2, b3 = _split3(b)
    return _dot(a_bf16, b1) + _dot(a_bf16, b2) + _dot(a_bf16, b3)


def _silu(x):
    return x * jax.nn.sigmoid(x)


def _tile_pos(t):
    is_ctx = t < NT_CTX
    j = jnp.maximum(t - NT_CTX, 0) % TILES_PER_LAT
    return is_ctx, j


def _lat_batch(t):
    return jnp.maximum(t - NT_CTX, 0) // TILES_PER_LAT


def _prev_halo(t):
    return jnp.maximum(t * (TILE // HALO) - 1, 0)


def _next_halo(t):
    return jnp.minimum((t + 1) * (TILE // HALO), N_HALO_BLOCKS - 1)


def _const_spec(shape):
    nd = len(shape)
    return pl.BlockSpec(shape, lambda *_: (0,) * nd, pipeline_mode=pl.Buffered(1))


def _layer_spec(shape, lidx):
    nd = len(shape)
    return pl.BlockSpec((1,) + tuple(shape), lambda *_: (lidx,) + (0,) * nd,
                        pipeline_mode=pl.Buffered(1))


def _row_spec(cols, fn=lambda i: i):
    return pl.BlockSpec((TILE, cols), lambda i: (fn(i), 0))


def _halo_spec(cols, which, fn=lambda i: i):
    pick = _prev_halo if which == "prev" else _next_halo
    return pl.BlockSpec((HALO, cols), lambda i: (pick(fn(i)), 0))


def _same_order(i):
    return i


def _lat_first_rows(i):
    return (i + NRT_CTX) % NRT


def _lat_first_tiles(i):
    return (i + NT_CTX) % NT


def _big_row_spec(cols, order=_same_order):
    return pl.BlockSpec((ROW_TILE, cols), lambda i: (order(i), 0))


def _big_halo_spec(cols, which, order=_same_order):
    per_tile = ROW_TILE // HALO
    if which == "prev":
        return pl.BlockSpec((HALO, cols), lambda i: (jnp.maximum(order(i) * per_tile - 1, 0), 0))
    return pl.BlockSpec((HALO, cols),
                        lambda i: (jnp.minimum((order(i) + 1) * per_tile, N_HALO_BLOCKS - 1), 0))


def _big_lat_index(t):
    rel = jnp.maximum(t - NRT_CTX, 0)
    return t < NRT_CTX, rel // ROW_TILES_PER_LAT, rel % ROW_TILES_PER_LAT


def _big_mod_spec(order=_same_order):
    def idx(i):
        is_ctx, b, _ = _big_lat_index(order(i))
        return (jnp.where(is_ctx, 0, 1 + b), 0, 0)
    return pl.BlockSpec((1, N_MOD, D_MODEL), idx)


def _params(n_axes=1):
    return pltpu.CompilerParams(dimension_semantics=("arbitrary",) * n_axes,
                                vmem_limit_bytes=VMEM_LIMIT)


def _rmsnorm_rows(x, g):
    ms = jnp.mean(x * x, axis=-1, keepdims=True)
    return x * lax.rsqrt(ms + EPS) * g


MOD_COLS = 1536


def _mod_kernel(c_ref, w_ref, b_ref, o_ref):
    s = _silu(c_ref[...]).astype(BF16)
    o_ref[0] = _dot(s, w_ref[0].astype(BF16)) + b_ref[0]


def _modulation(cvec, w_mod, b_mod):
    nmod = N_MOD * D_MODEL
    return pl.pallas_call(
        _mod_kernel,
        out_shape=jax.ShapeDtypeStruct((DEPTH, MOD_ROWS, nmod), F32),
        grid=(DEPTH, nmod // MOD_COLS),
        in_specs=[pl.BlockSpec((MOD_ROWS, D_MODEL), lambda l, j: (0, 0)),
                  pl.BlockSpec((1, D_MODEL, MOD_COLS), lambda l, j: (l, 0, j)),
                  pl.BlockSpec((1, 1, MOD_COLS), lambda l, j: (l, 0, j))],
        out_specs=pl.BlockSpec((1, MOD_ROWS, MOD_COLS), lambda l, j: (l, 0, j)),
        compiler_params=_params(2),
        name="modulation",
    )(cvec, w_mod, b_mod.reshape(DEPTH, 1, nmod))


_C_POOL = 0
_C_Q = _C_POOL + POOL_DIM
_C_K = _C_Q + ATT_QK_TOTAL
_C_V = _C_K + ATT_QK_TOTAL
_C_Z = _C_V + ATT_DIM
_C_XBC = _C_Z + SSD_DIM
_C_DT = _C_XBC + SSD_XBC_DIM
IN_DIM = _C_DT + 2 * SSD_HEADS
ROPE_SWAP = ATT_QK_DIM // 4
Q_PRESCALE = ATT_QK_DIM ** -0.5 * math.log2(math.e)


def _rope_block(xb, cos, sin):
    lane = lax.broadcasted_iota(jnp.int32, xb.shape, 1)
    first_half = (lane % (2 * ROPE_SWAP)) < ROPE_SWAP
    partner = jnp.where(first_half,
                        pltpu.roll(xb, LANE - ROPE_SWAP, axis=1),
                        pltpu.roll(xb, ROPE_SWAP, axis=1))
    return xb * cos + partner * sin


def _modulated_norm(x, g, mod_ref, shift_row):
    return (_rmsnorm_rows(x, g) * (1.0 + mod_ref[0, shift_row + 1:shift_row + 2, :])
            + mod_ref[0, shift_row:shift_row + 1, :])


N_INPROJ_IN = 9


def _inproj_kernel(*refs, has_cache_in):
    x_ref, xp_ref, xn_ref, mod_ref, g_ref, w_ref, wdt_ref, cos_ref, sin_ref = refs[:N_INPROJ_IN]
    outs = refs[N_INPROJ_IN + (2 if has_cache_in else 0):-1]
    pooled_ref, q_ref, kc_ref, vc_ref, kb_ref, vb_ref, z_ref, xbc_ref, dt_ref = outs
    pool_scr = refs[-1]
    is_ctx, _, j = _big_lat_index(_lat_first_rows(pl.program_id(0)))
    seqs = ROW_TILE // SEQ
    g = g_ref[0]
    hb = _modulated_norm(x_ref[...], g, mod_ref, 0).astype(BF16)
    w = w_ref.at[0]
    q = _dot(hb, w[:, _C_Q:_C_K])
    k = _dot(hb, w[:, _C_K:_C_V])
    cos = cos_ref[...]
    sin = sin_ref[...]
    for blk in range(ATT_QK_TOTAL // LANE):
        sl = slice(blk * LANE, (blk + 1) * LANE)
        q_ref[:, sl] = (_rope_block(q[:, sl], cos, sin) * Q_PRESCALE).astype(BF16)
        kr = _rope_block(k[:, sl], cos, sin)
        kc_ref[:, 0, :, sl] = kr.reshape(seqs, SEQ, LANE)
        kb_ref[:, sl] = kr.astype(BF16)
    h_halo = jnp.concatenate([_modulated_norm(xp_ref[...], g, mod_ref, 0),
                              _modulated_norm(xn_ref[...], g, mod_ref, 0)], axis=0).astype(BF16)
    u = _dot(jnp.concatenate([hb, h_halo], axis=0), w[:, _C_POOL:_C_Q])
    pooled_ref[...] = _pooled_deviation(u[0:ROW_TILE], u[ROW_TILE:ROW_TILE + HALO],
                                        u[ROW_TILE + HALO:], pool_scr, is_ctx, j)
    v = _dot(hb, w[:, _C_V:_C_Z])
    vc_ref[:, 0] = v.reshape(seqs, SEQ, ATT_DIM)
    vb_ref[...] = v.astype(BF16)
    z_ref[...] = _dot(hb, w[:, _C_Z:_C_XBC])
    dt_ref[...] = _dot(hb, wdt_ref[0])
    xbc_ref[...] = _dot(hb, w[:, _C_XBC:_C_DT])


def _inproj(x, mod_l, norm_g, w_in, w_dt, rope_cos, rope_sin, lidx, caches):
    order = _lat_first_rows
    seqs = ROW_TILE // SEQ
    cache_shape = (BATCH, DEPTH, SEQ, ATT_QK_TOTAL)
    row_outs = {0: (POOL_DIM, BF16), 1: (ATT_QK_TOTAL, BF16), 4: (ATT_QK_TOTAL, BF16), 5: (ATT_DIM, BF16),
                6: (SSD_DIM, F32), 7: (SSD_XBC_DIM, F32), 8: (DT_PAD, F32)}

    def cache_idx(i):
        t = order(i)
        return (jnp.where(t < NRT_CTX, t, 0), lidx, 0, 0)

    def rope_idx(i):
        is_ctx, _, j = _big_lat_index(order(i))
        return (jnp.where(is_ctx, 0, 1 + j), 0)

    out_shape, out_specs = [], []
    for n in range(9):
        if n in row_outs:
            c, d = row_outs[n]
            out_shape.append(jax.ShapeDtypeStruct((N_TOK, c), d))
            out_specs.append(_big_row_spec(c, order))
        else:
            out_shape.append(jax.ShapeDtypeStruct(cache_shape, F32))
            out_specs.append(pl.BlockSpec((seqs, 1, SEQ, ATT_QK_TOTAL), cache_idx))
    in_specs = [_big_row_spec(D_MODEL, order), _big_halo_spec(D_MODEL, "prev", order),
                _big_halo_spec(D_MODEL, "next", order), _big_mod_spec(order),
                _layer_spec((1, D_MODEL), lidx),
                _layer_spec((D_MODEL, IN_DIM), lidx),
                _layer_spec((D_MODEL, DT_PAD), lidx),
                pl.BlockSpec((ROW_TILE, LANE), rope_idx),
                pl.BlockSpec((ROW_TILE, LANE), rope_idx)]
    args = [x, x, x, mod_l, norm_g, w_in, w_dt, rope_cos, rope_sin]
    assert len(args) == N_INPROJ_IN
    aliases = {}
    if caches is not None:
        in_specs += [pl.BlockSpec(memory_space=pl.ANY)] * 2
        args += list(caches)
        aliases = {N_INPROJ_IN: 2, N_INPROJ_IN + 1: 3}
    return pl.pallas_call(
        functools.partial(_inproj_kernel, has_cache_in=caches is not None),
        out_shape=out_shape,
        grid=(NRT,),
        in_specs=in_specs,
        out_specs=out_specs,
        scratch_shapes=[pltpu.VMEM((POOL_SCR_ROWS, POOL_DIM), F32)],
        input_output_aliases=aliases,
        compiler_params=_params(),
        name="inproj",
    )(*args)


def _fill_halo_scratch(scr, prev_ref, cur, next_ref, first, last):
    scr[0:HALO, :] = jnp.where(first, 0.0, prev_ref[...])
    scr[HALO:HALO + TILE, :] = cur
    scr[HALO + TILE:, :] = jnp.where(last, 0.0, next_ref[...])


POOL_HALF_STRIDE = SEQ + 2 * HALO
POOL_SCR_ROWS = (ROW_TILE // SEQ) * POOL_HALF_STRIDE


def _pooled_deviation(u, u_prev, u_next, scr, is_ctx, j):
    n_half = ROW_TILE // SEQ
    first = jnp.logical_or(is_ctx, j == 0)
    last = jnp.logical_or(is_ctx, j == ROW_TILES_PER_LAT - 1)
    seq_len = jnp.where(is_ctx, SEQ, DEC_SEQ)
    for hf in range(n_half):
        base = hf * POOL_HALF_STRIDE
        if hf == 0:
            before = jnp.where(first, 0.0, u_prev)
        else:
            before = jnp.where(is_ctx, 0.0, u[hf * SEQ - HALO:hf * SEQ, :])
        if hf == n_half - 1:
            after = jnp.where(last, 0.0, u_next)
        else:
            after = jnp.where(is_ctx, 0.0, u[(hf + 1) * SEQ:(hf + 1) * SEQ + HALO, :])
        scr[base:base + HALO, :] = before
        scr[base + HALO:base + HALO + SEQ, :] = u[hf * SEQ:(hf + 1) * SEQ, :]
        scr[base + HALO + SEQ:base + POOL_HALF_STRIDE, :] = after
    halves = []
    for hf in range(n_half):
        base = hf * POOL_HALF_STRIDE + HALO
        pos = (jnp.where(is_ctx, 0, j * ROW_TILE + hf * SEQ)
               + lax.broadcasted_iota(jnp.int32, (SEQ, 1), 0))
        groups = []
        for g, win in enumerate(POOL_WINDOWS):
            half = win // 2
            sl = slice(g * POOL_GROUP_DIM, (g + 1) * POOL_GROUP_DIM)
            acc = scr[base - half:base - half + SEQ, sl]
            for k in range(1 - half, half):
                acc = acc + scr[base + k:base + k + SEQ, sl]
            lo = jnp.maximum(pos - half, 0)
            hi = jnp.minimum(pos + half - 1, seq_len - 1)
            inv_count = 1.0 / (hi - lo + 1).astype(F32)
            groups.append((acc * inv_count - scr[base:base + SEQ, sl]).astype(BF16))
        halves.append(jnp.concatenate(groups, axis=1))
    return jnp.concatenate(halves, axis=0)


def _attn_kernel(q_ref, kc_ref, vc_ref, kl_ref, vl_ref, kp_ref, vp_ref, lam_ref, g_ref, o_ref,
                 s_scr, *, lam_init):
    i = pl.program_id(0)
    lp = lam_ref[0]
    lam = (jnp.exp(jnp.sum(lp[0:1] * lp[1:2], axis=-1, keepdims=True))
           - jnp.exp(jnp.sum(lp[2:3] * lp[3:4], axis=-1, keepdims=True)) + lam_init)

    def stacked_queries(h):
        qh = q_ref[:, h * LANE:(h + 1) * LANE]
        lane = lax.broadcasted_iota(jnp.int32, qh.shape, 1)
        zero = jnp.zeros_like(qh)
        return jnp.concatenate([jnp.where(lane < ATT_QK_DIM, qh, zero),
                                jnp.where(lane >= ATT_QK_DIM, qh, zero)], axis=0)

    def run(key_blocks):
        def score_block(q2, h, b, buf, run_max):
            kr, _, r0 = key_blocks[b]
            s = _dot_nt(q2, kr[r0:r0 + KEY_BLOCK, h * LANE:(h + 1) * LANE])
            s_scr[buf, :, b * KEY_BLOCK:(b + 1) * KEY_BLOCK] = s
            blk_max = jnp.maximum(s[:, 0:LANE], s[:, LANE:KEY_BLOCK])
            return blk_max if run_max is None else jnp.maximum(run_max, blk_max)

        run_max = None
        q2 = stacked_queries(0)
        for b in range(len(key_blocks)):
            run_max = score_block(q2, 0, b, 0, run_max)
        for h in range(ATT_HEADS):
            sl = slice(h * LANE, (h + 1) * LANE)
            buf = h % 2
            m = jnp.max(run_max, axis=-1, keepdims=True)
            run_max, run_sum, acc = None, None, None
            if h + 1 < ATT_HEADS:
                q2 = stacked_queries(h + 1)
            for b, (_, vr, r0) in enumerate(key_blocks):
                if h + 1 < ATT_HEADS:
                    run_max = score_block(q2, h + 1, b, 1 - buf, run_max)
                p = jnp.exp2(s_scr[buf, :, b * KEY_BLOCK:(b + 1) * KEY_BLOCK] - m)
                blk_sum = p[:, 0:LANE] + p[:, LANE:KEY_BLOCK]
                run_sum = blk_sum if run_sum is None else run_sum + blk_sum
                part = _dot(p.astype(BF16), vr[r0:r0 + KEY_BLOCK, sl])
                acc = part if acc is None else acc + part
            o2 = acc / jnp.sum(run_sum, axis=-1, keepdims=True)
            oh = o2[0:TILE] - lam * o2[TILE:]
            oh = _rmsnorm_rows(oh, g_ref[0]) * (1.0 - lam_init)
            o_ref[:, sl] = oh.astype(BF16)

    @pl.when(i < NT_CTX)
    def _():
        run([(kc_ref, vc_ref, 0)])

    @pl.when(i >= NT_CTX)
    def _():
        run([(kp_ref.at[0, 0], vp_ref.at[0, 0], 0)]
            + [(kl_ref, vl_ref, r0) for r0 in range(0, DEC_SEQ, KEY_BLOCK)])


def _diff_attention(q, kb, vb, past_k, past_v, att_lambda, subln_g, lidx):
    lam_init = 0.8 - 0.6 * math.exp(-0.3 * lidx)
    ctx_idx = lambda i: (jnp.minimum(i, NT_CTX - 1), 0)
    lat_idx = lambda i: (N_CTX // DEC_SEQ + _lat_batch(i), 0)
    past_idx = lambda i: (_lat_batch(i), lidx, 0, 0)
    return pl.pallas_call(
        functools.partial(_attn_kernel, lam_init=lam_init),
        out_shape=jax.ShapeDtypeStruct((N_TOK, ATT_DIM), BF16),
        grid=(NT,),
        in_specs=[_row_spec(ATT_QK_TOTAL),
                  pl.BlockSpec((TILE, ATT_QK_TOTAL), ctx_idx),
                  pl.BlockSpec((TILE, ATT_DIM), ctx_idx),
                  pl.BlockSpec((DEC_SEQ, ATT_QK_TOTAL), lat_idx),
                  pl.BlockSpec((DEC_SEQ, ATT_DIM), lat_idx),
                  pl.BlockSpec((1, 1, PAST_LEN, ATT_QK_TOTAL), past_idx),
                  pl.BlockSpec((1, 1, PAST_LEN, ATT_DIM), past_idx),
                  _layer_spec((4, ATT_QK_DIM), lidx),
                  _layer_spec((1, ATT_V_DIM), lidx)],
        out_specs=_row_spec(ATT_DIM),
        scratch_shapes=[pltpu.VMEM((2, 2 * TILE, N_KEYS_LAT), F32)],
        compiler_params=_params(),
        name="diff_attention",
    )(q, kb, vb, kb, vb, past_k, past_v, att_lambda, subln_g)


def _ssd_scan_tile(direction, t, xc_scr, dt_ref, h0_ref, dtb_ref, asmall_ref, e_ref, st_ref,
                   state_scr, y_scr):
    is_ctx, j = _tile_pos(t)
    seq_begins = (j == 0) if direction == 0 else (j == TILES_PER_LAT - 1)

    @pl.when(is_ctx)
    def _():
        state_scr[...] = jnp.zeros_like(state_scr)

    @pl.when(jnp.logical_and(jnp.logical_not(is_ctx), seq_begins))
    def _():
        state_scr[...] = h0_ref[0, 0, 0].T

    a_small = -jnp.exp(asmall_ref[...])
    row = lax.broadcasted_iota(jnp.int32, (SSD_CHUNK, SSD_CHUNK), 0)
    col = lax.broadcasted_iota(jnp.int32, (SSD_CHUNK, SSD_CHUNK), 1)
    keep = (col <= row) if direction == 0 else (col >= row)
    tri = jnp.where(keep, 1.0, 0.0).astype(BF16)
    pair_lane = lax.broadcasted_iota(jnp.int32, (SSD_CHUNK, LANE), 1)
    edge = SSD_CHUNK - 1 if direction == 0 else 0
    heads_per_group = SSD_HEADS // SSD_GROUPS

    chunks = range(TILE // SSD_CHUNK)
    rows = [slice(c * SSD_CHUNK, (c + 1) * SSD_CHUNK) for c in chunks]
    groups = [slice(g * SSD_GROUP_DIM, (g + 1) * SSD_GROUP_DIM) for g in range(SSD_GROUPS)]
    dts = [jax.nn.softplus(dt_ref[r, :] + dtb_ref[...]) for r in rows]
    dt_exp = [_dot_split(dt, e_ref[...], 2) for dt in dts]
    acs_s = [_dot_exact_lhs(tri, dt * a_small) for dt in dts]
    acs_exp = [_dot_split(a, e_ref[...], 3) for a in acs_s]
    acs_st = [a.T for a in acs_s]
    b_mat = [[xc_scr[r, SSD_DIM + g * SSD_STATE:SSD_DIM + (g + 1) * SSD_STATE]
              for g in range(SSD_GROUPS)] for r in rows]
    c_mat = [[xc_scr[r, SSD_DIM + (SSD_GROUPS + g) * SSD_STATE:
                     SSD_DIM + (SSD_GROUPS + g + 1) * SSD_STATE].astype(BF16)
              for g in range(SSD_GROUPS)] for r in rows]
    cb = [[_dot_nt(c_mat[c][g], b_mat[c][g].astype(BF16)) for g in range(SSD_GROUPS)] for c in chunks]
    xdt = [xc_scr[rows[c], 0:SSD_DIM] * dt_exp[c] for c in chunks]
    total = [a[edge:edge + 1, :] for a in acs_exp]
    xw = [(xdt[c] * jnp.exp(total[c] - acs_exp[c])).astype(BF16) for c in chunks]
    decay_from_entry = [jnp.exp(a) for a in acs_exp]
    y_diag = []
    for c in chunks:
        per_group = []
        for g in range(SSD_GROUPS):
            pairs = []
            for pr in range(heads_per_group // 2):
                ps = slice(g * SSD_GROUP_DIM + pr * LANE, g * SSD_GROUP_DIM + (pr + 1) * LANE)
                x_pair = xdt[c][:, ps].astype(BF16)
                halves = []
                for hh in range(2):
                    hcol = direction * SSD_HEADS + g * heads_per_group + pr * 2 + hh
                    diff = acs_s[c][:, hcol:hcol + 1] - acs_st[c][hcol:hcol + 1, :]
                    decay = jnp.where(keep, jnp.exp(diff), 0.0)
                    halves.append(_dot((cb[c][g] * decay).astype(BF16), x_pair))
                pairs.append(jnp.where(pair_lane < SSD_HEAD_DIM, halves[0], halves[1]))
            per_group.append(jnp.concatenate(pairs, axis=1))
        y_diag.append(per_group)
    state_in = [_dot(b_mat[c][g].T.astype(BF16), xw[c][:, groups[g]])
                for c in chunks for g in range(SSD_GROUPS)]

    h = [state_scr[:, gs] for gs in groups]
    for c in (chunks if direction == 0 else reversed(chunks)):
        for g, gs in enumerate(groups):
            y_off = _dot(c_mat[c][g], h[g].astype(BF16)) * decay_from_entry[c][:, gs]
            y_scr[rows[c], gs] = y_diag[c][g] + y_off
            h[g] = h[g] * jnp.exp(total[c][:, gs]) + state_in[c * SSD_GROUPS + g]
    for g, gs in enumerate(groups):
        state_scr[:, gs] = h[g]
    st_ref[0, 0, 0] = jnp.concatenate(h, axis=1).T


N_SSD_FWD_IN = 10


def _ssd_fwd_kernel(*refs, has_state_in):
    (xbc_ref, xp_ref, xn_ref, dt_ref, h0_ref, cw_ref, cb_ref, dtb_ref, asmall_ref,
     e_ref) = refs[:N_SSD_FWD_IN]
    y_ref, xc_ref, st_ref, conv_scr, state_scr = refs[N_SSD_FWD_IN + (1 if has_state_in else 0):]
    t = _lat_first_tiles(pl.program_id(0))
    is_ctx, j = _tile_pos(t)
    first = jnp.logical_or(is_ctx, j == 0)
    last = jnp.logical_or(is_ctx, j == TILES_PER_LAT - 1)
    _fill_halo_scratch(conv_scr, xp_ref, xbc_ref[...], xn_ref, first, last)
    conv = (cw_ref[0, 0:1, :] * conv_scr[HALO - 1:HALO - 1 + TILE, :]
            + cw_ref[0, 1:2, :] * conv_scr[HALO:HALO + TILE, :]
            + cw_ref[0, 2:3, :] * conv_scr[HALO + 1:HALO + 1 + TILE, :] + cb_ref[0])
    xc_ref[...] = _silu(conv)
    _ssd_scan_tile(0, t, xc_ref, dt_ref, h0_ref, dtb_ref, asmall_ref, e_ref, st_ref, state_scr, y_ref)


def _ssd_bwd_kernel(xc_ref, dt_ref, h0_ref, dtb_ref, asmall_ref, e_ref, yf_ref, z_ref, dskip_ref, ng_ref,
                    st_in_ref, o_ref, st_ref, state_scr, y_scr):
    del st_in_ref
    t = NT - 1 - pl.program_id(0)
    _ssd_scan_tile(1, t, xc_ref, dt_ref, h0_ref, dtb_ref, asmall_ref, e_ref, st_ref, state_scr, y_scr)
    y = yf_ref[...] + y_scr[...] + dskip_ref[...] * xc_ref[:, 0:SSD_DIM]
    yz = y * _silu(z_ref[...])
    for g in range(SSD_GROUPS):
        gs = slice(g * SSD_GROUP_DIM, (g + 1) * SSD_GROUP_DIM)
        o_ref[:, gs] = _rmsnorm_rows(yz[:, gs], ng_ref[0, :, gs]).astype(BF16)


def _ssd_mixer(xbc, dt_raw, z, state_ssd, lidx, conv_w, conv_b, norm_g, p, new_state):
    st_shape = jax.ShapeDtypeStruct((BATCH, DEPTH, 2, SSD_DIM, SSD_STATE), F32)
    st_block = (1, 1, 1, SSD_DIM, SSD_STATE)
    rev = lambda i: NT - 1 - i
    fwd = _lat_first_tiles
    state_scratch = pltpu.VMEM((SSD_STATE, SSD_DIM), F32)
    any_spec = pl.BlockSpec(memory_space=pl.ANY)

    def st_idx(order, direction, parked):
        def idx(i):
            t = order(i)
            return (jnp.where(t < NT_CTX, t, parked), lidx, direction, 0, 0)
        return idx

    def scan_specs(order, direction):
        h0_idx = lambda i: (_lat_batch(order(i)), lidx, direction, 0, 0)
        return [_row_spec(DT_PAD, order),
                pl.BlockSpec((1, 1, 1, SSD_DIM, SSD_STATE), h0_idx),
                _const_spec((1, DT_PAD)), _const_spec((1, DT_PAD)), _const_spec((DT_PAD, SSD_DIM))]

    def scan_args(direction):
        return [dt_raw, state_ssd, p["dt_bias"], p["alog_small"][direction], p["expand"][direction]]

    fwd_specs = ([_row_spec(SSD_XBC_DIM, fwd), _halo_spec(SSD_XBC_DIM, "prev", fwd),
                  _halo_spec(SSD_XBC_DIM, "next", fwd)]
                 + scan_specs(fwd, 0)[:2]
                 + [_layer_spec((3, SSD_XBC_DIM), lidx), _layer_spec((1, SSD_XBC_DIM), lidx)]
                 + scan_specs(fwd, 0)[2:])
    fwd_args = [xbc, xbc, xbc, *scan_args(0)[:2], conv_w, conv_b, *scan_args(0)[2:]]
    assert len(fwd_args) == N_SSD_FWD_IN
    fwd_alias = {}
    if new_state is not None:
        fwd_specs.append(any_spec)
        fwd_args.append(new_state)
        fwd_alias = {N_SSD_FWD_IN: 2}
    y_f, xc, new_state = pl.pallas_call(
        functools.partial(_ssd_fwd_kernel, has_state_in=bool(fwd_alias)),
        out_shape=[jax.ShapeDtypeStruct((N_TOK, SSD_DIM), F32),
                   jax.ShapeDtypeStruct((N_TOK, SSD_XBC_DIM), F32), st_shape],
        grid=(NT,),
        in_specs=fwd_specs,
        out_specs=[_row_spec(SSD_DIM, fwd), _row_spec(SSD_XBC_DIM, fwd),
                   pl.BlockSpec(st_block, st_idx(fwd, 0, 0))],
        scratch_shapes=[pltpu.VMEM((TILE + 2 * HALO, SSD_XBC_DIM), F32), state_scratch],
        input_output_aliases=fwd_alias,
        compiler_params=_params(),
        name="ssd_forward",
    )(*fwd_args)

    bwd_args = [xc, *scan_args(1), y_f, z, p["d_skip"], norm_g, new_state]
    out, new_state = pl.pallas_call(
        _ssd_bwd_kernel,
        out_shape=[jax.ShapeDtypeStruct((N_TOK, SSD_DIM), BF16), st_shape],
        grid=(NT,),
        in_specs=([_row_spec(SSD_XBC_DIM, rev)] + scan_specs(rev, 1)
                  + [_row_spec(SSD_DIM, rev), _row_spec(SSD_DIM, rev),
                     _const_spec((1, SSD_DIM)), _layer_spec((1, SSD_DIM), lidx), any_spec]),
        out_specs=[_row_spec(SSD_DIM, rev), pl.BlockSpec(st_block, st_idx(rev, 1, NT_CTX - 1))],
        scratch_shapes=[state_scratch, pltpu.VMEM((TILE, SSD_DIM), F32)],
        input_output_aliases={len(bwd_args) - 1: 1},
        compiler_params=_params(),
        name="ssd_backward",
    )(*bwd_args)
    return out, new_state


def _merge_kernel(x_ref, mod_ref, g_ref, wg_ref, bg_ref, pooled_ref, pw_ref, ps_ref,
                  oa_ref, os_ref, wb_ref, wo_ref, o_ref):
    x = x_ref[...]
    hb = _modulated_norm(x, g_ref[0], mod_ref, 0).astype(BF16)
    wg = wg_ref.at[0]
    wb = wb_ref.at[0]

    def gated(n, branch, row0, width):
        cols = slice(n * D_MODEL, (n + 1) * D_MODEL)
        gate = jax.nn.sigmoid(_dot(hb, wg[:, cols]) + bg_ref[0, :, cols])
        return gate * _dot(branch, wb[row0:row0 + width, :])

    o_pool = jnp.concatenate(
        [_dot(pooled_ref[:, g * POOL_GROUP_DIM:(g + 1) * POOL_GROUP_DIM], pw_ref[0, g])
         for g in range(POOL_GROUPS)], axis=1) * ps_ref[0]
    merged = (gated(0, o_pool.astype(BF16), 0, POOL_DIM)
              + gated(1, oa_ref[...], POOL_DIM, ATT_DIM)
              + gated(2, os_ref[...], POOL_DIM + ATT_DIM, SSD_DIM))
    o_ref[...] = x + mod_ref[0, 2:3, :] * _dot(merged.astype(BF16), wo_ref[0])


def _merge(x, mod_l, norm_g, w_gate, b_gate, pooled, pool_w, pool_scale, o_att, o_ssd, w_branch, w_out,
           lidx):
    return pl.pallas_call(
        _merge_kernel,
        out_shape=jax.ShapeDtypeStruct((N_TOK, D_MODEL), F32),
        grid=(NRT,),
        in_specs=[_big_row_spec(D_MODEL), _big_mod_spec(),
                  _layer_spec((1, D_MODEL), lidx),
                  _layer_spec((D_MODEL, 3 * D_MODEL), lidx), _layer_spec((1, 3 * D_MODEL), lidx),
                  _big_row_spec(POOL_DIM),
                  _layer_spec((POOL_GROUPS, POOL_GROUP_DIM, POOL_GROUP_DIM), lidx),
                  _layer_spec((1, POOL_DIM), lidx),
                  _big_row_spec(ATT_DIM), _big_row_spec(SSD_DIM),
                  _layer_spec((MIX_DIM, D_MODEL), lidx), _layer_spec((D_MODEL, D_MODEL), lidx)],
        out_specs=_big_row_spec(D_MODEL),
        compiler_params=_params(),
        name="merge",
    )(x, mod_l, norm_g, w_gate, b_gate, pooled, pool_w, pool_scale, o_att, o_ssd, w_branch, w_out)


N_FFN_IN = 10


def _ffn_kernel(*refs, final):
    x_ref, xp_ref, xn_ref, mod_ref, g_ref, wu_ref, cw_ref, cb_ref, wd_ref, fg_ref = refs[:N_FFN_IN]
    h_scr, cols_scr = refs[-2:]
    out_refs = refs[N_FFN_IN:-2]
    res_ref = h_scr if final else out_refs[0]
    i = pl.program_id(0)
    is_ctx, _, j = _big_lat_index(i)
    first = jnp.logical_or(is_ctx, j == 0)
    last = jnp.logical_or(is_ctx, j == ROW_TILES_PER_LAT - 1)
    seq_len = jnp.where(is_ctx, SEQ, DEC_SEQ)
    tile_pos = jnp.where(is_ctx, 0, j * ROW_TILE)
    n_col = D_MODEL // LANE

    g = g_ref[0]
    h_nat = _modulated_norm(x_ref[...], g, mod_ref, 3)
    for cb in range(n_col):
        cols_scr[cb] = h_nat[:, cb * LANE:(cb + 1) * LANE]
    for r in range(ROW_GROUPS):
        for cb in range(n_col):
            h_scr[r * SUBLANE:(r + 1) * SUBLANE, cb * LANE:(cb + 1) * LANE] = (
                cols_scr[cb, pl.ds(r, SUBLANE, stride=ROW_GROUPS), :])
    h_scr[ROW_TILE:ROW_TILE + HALO, :] = jnp.where(first, 0.0, _modulated_norm(xp_ref[...], g, mod_ref, 3))
    h_scr[ROW_TILE + HALO:, :] = jnp.where(last, 0.0, _modulated_norm(xn_ref[...], g, mod_ref, 3))
    hb = h_scr[...].astype(BF16)

    sub = lax.broadcasted_iota(jnp.int32, (SUBLANE, 1), 0)
    pos_first = tile_pos + sub * ROW_GROUPS
    pos_last = pos_first + (ROW_GROUPS - 1)
    at_seq_start = (pos_first % seq_len) == 0
    at_seq_end = (pos_last % seq_len) == seq_len - 1
    wu = wu_ref.at[0]

    def up_proj(c):
        return (_dot(hb, wu[:, c * FFN_CHUNK:(c + 1) * FFN_CHUNK]),
                _dot(hb, wu[:, FFN_DIM + c * FFN_CHUNK:FFN_DIM + (c + 1) * FFN_CHUNK]))

    def conv_half(up, col0):
        cols = slice(col0, col0 + FFN_CHUNK)
        a = up[0:ROW_TILE]
        halo_prev = up[ROW_TILE + HALO - 1:ROW_TILE + HALO]
        halo_next = up[ROW_TILE + HALO:ROW_TILE + HALO + 1]
        wrap_prev = pltpu.roll(a[ROW_TILE - SUBLANE:ROW_TILE], 1, axis=0)
        wrap_prev = jnp.where(sub == 0, halo_prev, wrap_prev)
        wrap_prev = jnp.where(at_seq_start, 0.0, wrap_prev)
        wrap_next = pltpu.roll(a[0:SUBLANE], SUBLANE - 1, axis=0)
        wrap_next = jnp.where(sub == SUBLANE - 1, halo_next, wrap_next)
        wrap_next = jnp.where(at_seq_end, 0.0, wrap_next)
        before = jnp.concatenate([wrap_prev, a[0:ROW_TILE - SUBLANE]], axis=0)
        after = jnp.concatenate([a[SUBLANE:ROW_TILE], wrap_next], axis=0)
        return (cw_ref[0, 0:1, cols] * before + cw_ref[0, 1:2, cols] * a
                + cw_ref[0, 2:3, cols] * after + cb_ref[0, :, cols])

    acc = None
    n_chunks = FFN_DIM // FFN_CHUNK
    up_next = up_proj(0)
    for c in range(n_chunks):
        up_gate, up_val = up_next
        if c + 1 < n_chunks:
            up_next = up_proj(c + 1)
        a_gate = conv_half(up_gate, c * FFN_CHUNK)
        a_val = conv_half(up_val, FFN_DIM + c * FFN_CHUNK)
        act = (_silu(a_gate) * a_val).astype(BF16)
        part = _dot(act, wd_ref[0, c * FFN_CHUNK:(c + 1) * FFN_CHUNK, :])
        acc = part if acc is None else acc + part
    upd = mod_ref[0, 5:6, :] * acc
    for cb in range(n_col):
        cols_scr[cb] = upd[:, cb * LANE:(cb + 1) * LANE]
    groups_per_sub = ROW_GROUPS // SUBLANE
    for q in range(ROW_GROUPS):
        start = (q % groups_per_sub) * ROW_GROUPS + q // groups_per_sub
        rows = slice(q * SUBLANE, (q + 1) * SUBLANE)
        for cb in range(n_col):
            cs = slice(cb * LANE, (cb + 1) * LANE)
            res_ref[rows, cs] = x_ref[rows, cs] + cols_scr[cb, pl.ds(start, SUBLANE, stride=SUBLANE), :]
    if final:
        y = _rmsnorm_rows(res_ref[0:ROW_TILE, :], fg_ref[...])
        y_ctx_ref, y_lat_ref = out_refs

        @pl.when(is_ctx)
        def _():
            y_ctx_ref[...] = y

        @pl.when(jnp.logical_not(is_ctx))
        def _():
            y_lat_ref[...] = y


def _conv_ffn(x, mod_l, norm_g, w_up, conv_w, conv_b, w_down, final_g, lidx, final):
    if final:
        out_shape = [jax.ShapeDtypeStruct((N_CTX, D_MODEL), F32), jax.ShapeDtypeStruct((N_LAT, D_MODEL), F32)]
        out_specs = [pl.BlockSpec((ROW_TILE, D_MODEL), lambda i: (jnp.minimum(i, NRT_CTX - 1), 0)),
                     pl.BlockSpec((ROW_TILE, D_MODEL), lambda i: (jnp.maximum(i - NRT_CTX, 0), 0))]
    else:
        out_shape = jax.ShapeDtypeStruct((N_TOK, D_MODEL), F32)
        out_specs = _big_row_spec(D_MODEL)
    return pl.pallas_call(
        functools.partial(_ffn_kernel, final=final),
        out_shape=out_shape,
        grid=(NRT,),
        in_specs=[_big_row_spec(D_MODEL), _big_halo_spec(D_MODEL, "prev"), _big_halo_spec(D_MODEL, "next"),
                  _big_mod_spec(),
                  _layer_spec((1, D_MODEL), lidx),
                  _layer_spec((D_MODEL, 2 * FFN_DIM), lidx), _layer_spec((3, 2 * FFN_DIM), lidx),
                  _layer_spec((1, 2 * FFN_DIM), lidx), _layer_spec((FFN_DIM, D_MODEL), lidx),
                  _const_spec((1, D_MODEL))],
        out_specs=out_specs,
        scratch_shapes=[pltpu.VMEM((ROW_TILE + 2 * HALO, D_MODEL), F32),
                        pltpu.VMEM((D_MODEL // LANE, ROW_TILE, LANE), F32)],
        compiler_params=_params(),
        name="conv_ffn",
    )(x, x, x, mod_l, norm_g, w_up, conv_w, conv_b, w_down, final_g)


def _rope_tables():
    rows = DEC_SEQ // GRID_W
    row = jnp.repeat(jnp.arange(rows), GRID_W).astype(F32)
    col = jnp.tile(jnp.arange(GRID_W), rows).astype(F32)
    half = ATT_QK_DIM // 2
    inv = ROPE_BASE ** (-jnp.arange(0, half, 2, dtype=F32) / half)
    ar, ac = row[:, None] * inv, col[:, None] * inv
    cos = jnp.concatenate([jnp.cos(ar), jnp.cos(ar), jnp.cos(ac), jnp.cos(ac)], axis=-1)
    sin = jnp.concatenate([-jnp.sin(ar), jnp.sin(ar), -jnp.sin(ac), jnp.sin(ac)], axis=-1)
    reps = LANE // ATT_QK_DIM
    cos = jnp.concatenate([jnp.ones((ROW_TILE, LANE), F32), jnp.tile(cos, (1, reps))], axis=0)
    sin = jnp.concatenate([jnp.zeros((ROW_TILE, LANE), F32), jnp.tile(sin, (1, reps))], axis=0)
    return cos, sin


def _expand_matrices():
    e = np.zeros((2, DT_PAD, SSD_DIM), np.float32)
    for d in range(2):
        for h in range(SSD_HEADS):
            e[d, d * SSD_HEADS + h, h * SSD_HEAD_DIM:(h + 1) * SSD_HEAD_DIM] = 1.0
    return jnp.asarray(e, BF16)


def _ssd_small_params(l, ssd_dt_bias, ssd_A_log, ssd_D):
    n_dt = 2 * SSD_HEADS
    alog = ssd_A_log[l]
    alog_small = jnp.zeros((2, 1, DT_PAD), F32)
    for d in range(2):
        alog_small = alog_small.at[d, 0, d * SSD_HEADS:(d + 1) * SSD_HEADS].set(alog[d])
    return dict(
        dt_bias=jnp.pad(ssd_dt_bias[l].reshape(1, n_dt), ((0, 0), (0, DT_PAD - n_dt))),
        alog_small=alog_small,
        expand=_expand_matrices(),
        d_skip=jnp.repeat(ssd_D[l], SSD_HEAD_DIM)[None],
    )


def kernel(x_prompt, x_sample, c, cache_attn_k, cache_attn_v, state_ssd, c_ctx, w_mod, b_mod, norm1_g, w_in, pool_w, pool_scale, att_lambda, att_subln_g, ssd_conv_w, ssd_conv_b, ssd_dt_bias, ssd_A_log, ssd_D, ssd_norm_g, w_gate, b_gate, w_branch, w_out, norm2_g, ffn_w_up, ffn_conv_w, ffn_conv_b, ffn_w_down, final_norm_g):
    x = jnp.concatenate([x_prompt.reshape(N_CTX, D_MODEL), x_sample.reshape(N_LAT, D_MODEL)], axis=0)
    cvec = jnp.concatenate([c_ctx[None], c, jnp.zeros((MOD_ROWS - 1 - DEC_BATCH, D_MODEL), F32)], axis=0)
    mod = _modulation(cvec, w_mod, b_mod).reshape(DEPTH, MOD_ROWS, N_MOD, D_MODEL)
    rope_cos, rope_sin = _rope_tables()
    past_k = cache_attn_k.reshape(DEC_BATCH, DEPTH, PAST_LEN, ATT_QK_TOTAL).astype(BF16)
    past_v = cache_attn_v.reshape(DEC_BATCH, DEPTH, PAST_LEN, ATT_DIM).astype(BF16)
    past_state = state_ssd.reshape(DEC_BATCH, DEPTH, 2, SSD_DIM, SSD_STATE)

    n_dt = 2 * SSD_HEADS
    w_in_b = w_in.astype(BF16)
    w_dt_b = jnp.pad(w_in[:, :, _C_DT:], ((0, 0), (0, 0), (0, DT_PAD - n_dt))).astype(BF16)
    w_gate_b, w_branch_b, w_out_b = w_gate.astype(BF16), w_branch.astype(BF16), w_out.astype(BF16)
    w_up_b, w_down_b, pool_w_b = ffn_w_up.astype(BF16), ffn_w_down.astype(BF16), pool_w.astype(BF16)
    row = lambda a: a[:, None, :]
    norm1_r, norm2_r, b_gate_r, pool_scale_r = row(norm1_g), row(norm2_g), row(b_gate), row(pool_scale)
    subln_r, conv_b_r, ssd_norm_r, ffn_conv_b_r = row(att_subln_g), row(ssd_conv_b), row(ssd_norm_g), row(ffn_conv_b)
    final_g = final_norm_g[None]

    caches, new_state = None, None
    for l in range(DEPTH):
        pooled, q, k_cache, v_cache, kb, vb, z, xbc, dt_raw = _inproj(x, mod[l], norm1_r, w_in_b, w_dt_b,
                                                                      rope_cos, rope_sin, l, caches)
        caches = (k_cache, v_cache)
        o_att = _diff_attention(q, kb, vb, past_k, past_v, att_lambda, subln_r, l)
        o_ssd, new_state = _ssd_mixer(xbc, dt_raw, z, past_state, l, ssd_conv_w, conv_b_r, ssd_norm_r,
                                      _ssd_small_params(l, ssd_dt_bias, ssd_A_log, ssd_D), new_state)
        x = _merge(x, mod[l], norm1_r, w_gate_b, b_gate_r, pooled, pool_w_b, pool_scale_r, o_att, o_ssd,
                   w_branch_b, w_out_b, l)
        x = _conv_ffn(x, mod[l], norm2_r, w_up_b, ffn_conv_w, ffn_conv_b_r, w_down_b, final_g, l,
                      final=(l == DEPTH - 1))

    y_ctx, y_lat = x
    return (y_ctx.reshape(BATCH, SEQ, D_MODEL), y_lat.reshape(DEC_BATCH, DEC_SEQ, D_MODEL),
            caches[0].reshape(BATCH, DEPTH, SEQ, ATT_HEADS, 2, ATT_QK_DIM),
            caches[1].reshape(BATCH, DEPTH, SEQ, ATT_HEADS, ATT_V_DIM),
            new_state.reshape(BATCH, DEPTH, 2, SSD_HEADS, SSD_HEAD_DIM, SSD_STATE))
```

```python
import functools
import math

import numpy as np
import jax
import jax.numpy as jnp
from jax import lax
from jax.experimental import pallas as pl
from jax.experimental.pallas import tpu as pltpu

D_MODEL = 1024
BATCH = 32
SEQ = 256
DEPTH = 4
DEC_BATCH = 4
DEC_SEQ = 2048
PAST_LEN = 256
GRID_W = 64
EPS = 1e-6
N_MOD = 6

POOL_GROUPS = 4
POOL_WINDOWS = (2, 4, 8, 16)
POOL_DIM = D_MODEL // 2
POOL_GROUP_DIM = POOL_DIM // POOL_GROUPS

ATT_HEADS = 4
ATT_QK_DIM = 64
ATT_V_DIM = 2 * ATT_QK_DIM
ATT_QK_TOTAL = ATT_HEADS * 2 * ATT_QK_DIM
ATT_DIM = ATT_HEADS * ATT_V_DIM
ROPE_BASE = 10000.0

SSD_DIM = D_MODEL // 2
SSD_HEAD_DIM = 64
SSD_HEADS = SSD_DIM // SSD_HEAD_DIM
SSD_GROUPS = 2
SSD_STATE = 128
SSD_CHUNK = 128
SSD_XBC_DIM = SSD_DIM + 2 * SSD_GROUPS * SSD_STATE
SSD_GROUP_DIM = SSD_DIM // SSD_GROUPS

MIX_DIM = POOL_DIM + ATT_DIM + SSD_DIM
FFN_DIM = ((8 * D_MODEL // 3 + 127) // 128) * 128

LANE = 128
SUBLANE = 8
HALO = SUBLANE

TILE = 256
N_CTX = BATCH * SEQ
N_LAT = DEC_BATCH * DEC_SEQ
N_TOK = N_CTX + N_LAT
NT_CTX = N_CTX // TILE
NT = N_TOK // TILE
TILES_PER_LAT = DEC_SEQ // TILE
N_HALO_BLOCKS = N_TOK // HALO
MOD_ROWS = 8
DT_PAD = LANE
FFN_CHUNK = FFN_DIM // 2
VMEM_LIMIT = 56 * 1024 * 1024

ROW_TILE = 512
NRT = N_TOK // ROW_TILE
NRT_CTX = N_CTX // ROW_TILE
ROW_TILES_PER_LAT = DEC_SEQ // ROW_TILE
ROW_GROUPS = ROW_TILE // SUBLANE

KEY_BLOCK = 256
N_KEYS_LAT = PAST_LEN + DEC_SEQ

F32 = jnp.float32
BF16 = jnp.bfloat16

assert SEQ == TILE and DEC_SEQ % TILE == 0 and PAST_LEN == TILE
assert TILE == 2 * SSD_CHUNK and SSD_STATE == LANE and FFN_CHUNK % LANE == 0
assert ROW_TILE % SEQ == 0 and DEC_SEQ % ROW_TILE == 0
assert KEY_BLOCK == PAST_LEN == SEQ and DEC_SEQ % KEY_BLOCK == 0 and KEY_BLOCK == 2 * LANE


def _dot(a, b):
    return jnp.dot(a, b, preferred_element_type=F32)


def _dot_nt(a, b):
    return lax.dot_general(a, b, (((1,), (1,)), ((), ())), preferred_element_type=F32)


def _split3(x):
    x1 = x.astype(BF16)
    r1 = x - x1.astype(F32)
    x2 = r1.astype(BF16)
    x3 = (r1 - x2.astype(F32)).astype(BF16)
    return x1, x2, x3


def _dot_split(a, b_bf16, terms):
    acc = None
    for piece in _split3(a)[:terms]:
        part = _dot(piece, b_bf16)
        acc = part if acc is None else acc + part
    return acc


def _dot_exact_lhs(a_bf16, b):
    b1, b2, b3 = _split3(b)
    return _dot(a_bf16, b1) + _dot(a_bf16, b2) + _dot(a_bf16, b3)


def _silu(x):
    return x * jax.nn.sigmoid(x)


def _lat_batch(t):
    return jnp.maximum(t - NT_CTX, 0) // TILES_PER_LAT


def _const_spec(shape):
    nd = len(shape)
    return pl.BlockSpec(shape, lambda *_: (0,) * nd, pipeline_mode=pl.Buffered(1))


def _layer_spec(shape, lidx):
    nd = len(shape)
    return pl.BlockSpec((1,) + tuple(shape), lambda *_: (lidx,) + (0,) * nd,
                        pipeline_mode=pl.Buffered(1))


def _row_spec(cols):
    return pl.BlockSpec((TILE, cols), lambda i: (i, 0))


def _same_order(i):
    return i


def _lat_first_rows(i):
    return (i + NRT_CTX) % NRT


def _big_row_spec(cols, order=_same_order):
    return pl.BlockSpec((ROW_TILE, cols), lambda i: (order(i), 0))


def _big_halo_spec(cols, which, order=_same_order):
    per_tile = ROW_TILE // HALO
    if which == "prev":
        return pl.BlockSpec((HALO, cols), lambda i: (jnp.maximum(order(i) * per_tile - 1, 0), 0))
    return pl.BlockSpec((HALO, cols),
                        lambda i: (jnp.minimum((order(i) + 1) * per_tile, N_HALO_BLOCKS - 1), 0))


def _big_lat_index(t):
    rel = jnp.maximum(t - NRT_CTX, 0)
    return t < NRT_CTX, rel // ROW_TILES_PER_LAT, rel % ROW_TILES_PER_LAT


def _big_mod_spec(order=_same_order):
    def idx(i):
        is_ctx, b, _ = _big_lat_index(order(i))
        return (jnp.where(is_ctx, 0, 1 + b), 0, 0)
    return pl.BlockSpec((1, N_MOD, D_MODEL), idx)


def _params(n_axes=1):
    return pltpu.CompilerParams(dimension_semantics=("arbitrary",) * n_axes,
                                vmem_limit_bytes=VMEM_LIMIT)


def _rmsnorm_rows(x, g):
    ms = jnp.mean(x * x, axis=-1, keepdims=True)
    return x * lax.rsqrt(ms + EPS) * g


MOD_COLS = 1536


def _mod_kernel(c_ref, w_ref, b_ref, o_ref):
    s = _silu(c_ref[...]).astype(BF16)
    o_ref[0] = _dot(s, w_ref[0].astype(BF16)) + b_ref[0]


def _modulation(cvec, w_mod, b_mod):
    nmod = N_MOD * D_MODEL
    return pl.pallas_call(
        _mod_kernel,
        out_shape=jax.ShapeDtypeStruct((DEPTH, MOD_ROWS, nmod), F32),
        grid=(DEPTH, nmod // MOD_COLS),
        in_specs=[pl.BlockSpec((MOD_ROWS, D_MODEL), lambda l, j: (0, 0)),
                  pl.BlockSpec((1, D_MODEL, MOD_COLS), lambda l, j: (l, 0, j)),
                  pl.BlockSpec((1, 1, MOD_COLS), lambda l, j: (l, 0, j))],
        out_specs=pl.BlockSpec((1, MOD_ROWS, MOD_COLS), lambda l, j: (l, 0, j)),
        compiler_params=_params(2),
        name="modulation",
    )(cvec, w_mod, b_mod.reshape(DEPTH, 1, nmod))


_C_POOL = 0
_C_Q = _C_POOL + POOL_DIM
_C_K = _C_Q + ATT_QK_TOTAL
_C_V = _C_K + ATT_QK_TOTAL
_C_Z = _C_V + ATT_DIM
_C_XBC = _C_Z + SSD_DIM
_C_DT = _C_XBC + SSD_XBC_DIM
IN_DIM = _C_DT + 2 * SSD_HEADS
ROPE_SWAP = ATT_QK_DIM // 4
Q_PRESCALE = ATT_QK_DIM ** -0.5 * math.log2(math.e)


def _rope_block(xb, cos, sin):
    lane = lax.broadcasted_iota(jnp.int32, xb.shape, 1)
    first_half = (lane % (2 * ROPE_SWAP)) < ROPE_SWAP
    partner = jnp.where(first_half,
                        pltpu.roll(xb, LANE - ROPE_SWAP, axis=1),
                        pltpu.roll(xb, ROPE_SWAP, axis=1))
    return xb * cos + partner * sin


def _modulated_norm(x, g, mod_ref, shift_row):
    return (_rmsnorm_rows(x, g) * (1.0 + mod_ref[0, shift_row + 1:shift_row + 2, :])
            + mod_ref[0, shift_row:shift_row + 1, :])


def _inproj_kernel(x_ref, xp_ref, xn_ref, mod_ref, g_ref, w_ref, wdt_ref, cos_ref, sin_ref,
                   kc_in_ref, vc_in_ref,
                   pooled_ref, q_ref, kc_ref, vc_ref, kb_ref, vb_ref, z_ref, xbc_ref, dt_ref, pool_scr):
    del kc_in_ref, vc_in_ref
    is_ctx, _, j = _big_lat_index(_lat_first_rows(pl.program_id(0)))
    seqs = ROW_TILE // SEQ
    g = g_ref[0]
    hb = _modulated_norm(x_ref[...], g, mod_ref, 0).astype(BF16)
    w = w_ref.at[0]
    q = _dot(hb, w[:, _C_Q:_C_K])
    k = _dot(hb, w[:, _C_K:_C_V])
    cos = cos_ref[...]
    sin = sin_ref[...]
    for blk in range(ATT_QK_TOTAL // LANE):
        sl = slice(blk * LANE, (blk + 1) * LANE)
        q_ref[:, sl] = (_rope_block(q[:, sl], cos, sin) * Q_PRESCALE).astype(BF16)
        kr = _rope_block(k[:, sl], cos, sin)
        kc_ref[:, 0, :, sl] = kr.reshape(seqs, SEQ, LANE)
        kb_ref[:, sl] = kr.astype(BF16)
    h_halo = jnp.concatenate([_modulated_norm(xp_ref[...], g, mod_ref, 0),
                              _modulated_norm(xn_ref[...], g, mod_ref, 0)], axis=0).astype(BF16)
    u = _dot(jnp.concatenate([hb, h_halo], axis=0), w[:, _C_POOL:_C_Q])
    pooled_ref[...] = _pooled_deviation(u[0:ROW_TILE], u[ROW_TILE:ROW_TILE + HALO],
                                        u[ROW_TILE + HALO:], pool_scr, is_ctx, j)
    v = _dot(hb, w[:, _C_V:_C_Z])
    vc_ref[:, 0] = v.reshape(seqs, SEQ, ATT_DIM)
    vb_ref[...] = v.astype(BF16)
    z_ref[...] = _dot(hb, w[:, _C_Z:_C_XBC])
    dt_ref[...] = _dot(hb, wdt_ref[0])
    xbc_ref[...] = _dot(hb, w[:, _C_XBC:_C_DT])


def _inproj(x, mod_l, norm_g, w_in, w_dt, rope_cos, rope_sin, lidx, caches):
    order = _lat_first_rows
    seqs = ROW_TILE // SEQ
    cache_shape = (BATCH, DEPTH, SEQ, ATT_QK_TOTAL)
    row_outs = {0: (POOL_DIM, BF16), 1: (ATT_QK_TOTAL, BF16), 4: (ATT_QK_TOTAL, BF16), 5: (ATT_DIM, BF16),
                6: (SSD_DIM, F32), 7: (SSD_XBC_DIM, F32), 8: (DT_PAD, F32)}

    def cache_idx(i):
        t = order(i)
        return (jnp.where(t < NRT_CTX, t, 0), lidx, 0, 0)

    def rope_idx(i):
        is_ctx, _, j = _big_lat_index(order(i))
        return (jnp.where(is_ctx, 0, 1 + j), 0)

    out_shape, out_specs = [], []
    for n in range(9):
        if n in row_outs:
            c, d = row_outs[n]
            out_shape.append(jax.ShapeDtypeStruct((N_TOK, c), d))
            out_specs.append(_big_row_spec(c, order))
        else:
            out_shape.append(jax.ShapeDtypeStruct(cache_shape, F32))
            out_specs.append(pl.BlockSpec((seqs, 1, SEQ, ATT_QK_TOTAL), cache_idx))
    in_specs = [_big_row_spec(D_MODEL, order), _big_halo_spec(D_MODEL, "prev", order),
                _big_halo_spec(D_MODEL, "next", order), _big_mod_spec(order),
                _layer_spec((1, D_MODEL), lidx),
                _layer_spec((D_MODEL, IN_DIM), lidx),
                _layer_spec((D_MODEL, DT_PAD), lidx),
                pl.BlockSpec((ROW_TILE, LANE), rope_idx),
                pl.BlockSpec((ROW_TILE, LANE), rope_idx)]
    in_specs += [pl.BlockSpec(memory_space=pl.ANY)] * 2
    args = [x, x, x, mod_l, norm_g, w_in, w_dt, rope_cos, rope_sin, *caches]
    return pl.pallas_call(
        _inproj_kernel,
        out_shape=out_shape,
        grid=(NRT,),
        in_specs=in_specs,
        out_specs=out_specs,
        scratch_shapes=[pltpu.VMEM((POOL_SCR_ROWS, POOL_DIM), F32)],
        input_output_aliases={len(args) - 2: 2, len(args) - 1: 3},
        compiler_params=_params(),
        name="inproj",
    )(*args)


POOL_HALF_STRIDE = SEQ + 2 * HALO
POOL_SCR_ROWS = (ROW_TILE // SEQ) * POOL_HALF_STRIDE


def _fill_halves(scr, cur, prev, nxt, is_ctx, j):
    n_half = ROW_TILE // SEQ
    first = jnp.logical_or(is_ctx, j == 0)
    last = jnp.logical_or(is_ctx, j == ROW_TILES_PER_LAT - 1)
    for hf in range(n_half):
        base = hf * POOL_HALF_STRIDE
        if hf == 0:
            before = jnp.where(first, 0.0, prev)
        else:
            before = jnp.where(is_ctx, 0.0, cur[hf * SEQ - HALO:hf * SEQ, :])
        if hf == n_half - 1:
            after = jnp.where(last, 0.0, nxt)
        else:
            after = jnp.where(is_ctx, 0.0, cur[(hf + 1) * SEQ:(hf + 1) * SEQ + HALO, :])
        scr[base:base + HALO, :] = before
        scr[base + HALO:base + HALO + SEQ, :] = cur[hf * SEQ:(hf + 1) * SEQ, :]
        scr[base + HALO + SEQ:base + POOL_HALF_STRIDE, :] = after


def _pooled_deviation(u, u_prev, u_next, scr, is_ctx, j):
    n_half = ROW_TILE // SEQ
    seq_len = jnp.where(is_ctx, SEQ, DEC_SEQ)
    _fill_halves(scr, u, u_prev, u_next, is_ctx, j)
    halves = []
    for hf in range(n_half):
        base = hf * POOL_HALF_STRIDE + HALO
        pos = (jnp.where(is_ctx, 0, j * ROW_TILE + hf * SEQ)
               + lax.broadcasted_iota(jnp.int32, (SEQ, 1), 0))
        groups = []
        for g, win in enumerate(POOL_WINDOWS):
            half = win // 2
            sl = slice(g * POOL_GROUP_DIM, (g + 1) * POOL_GROUP_DIM)
            acc = scr[base - half:base - half + SEQ, sl]
            for k in range(1 - half, half):
                acc = acc + scr[base + k:base + k + SEQ, sl]
            lo = jnp.maximum(pos - half, 0)
            hi = jnp.minimum(pos + half - 1, seq_len - 1)
            inv_count = 1.0 / (hi - lo + 1).astype(F32)
            groups.append((acc * inv_count - scr[base:base + SEQ, sl]).astype(BF16))
        halves.append(jnp.concatenate(groups, axis=1))
    return jnp.concatenate(halves, axis=0)


def _attn_kernel(q_ref, kc_ref, vc_ref, kl_ref, vl_ref, kp_ref, vp_ref, lam_ref, g_ref, o_ref,
                 s_scr, *, lam_init):
    i = pl.program_id(0)
    lp = lam_ref[0]
    lam = (jnp.exp(jnp.sum(lp[0:1] * lp[1:2], axis=-1, keepdims=True))
           - jnp.exp(jnp.sum(lp[2:3] * lp[3:4], axis=-1, keepdims=True)) + lam_init)

    def stacked_queries(h):
        qh = q_ref[:, h * LANE:(h + 1) * LANE]
        lane = lax.broadcasted_iota(jnp.int32, qh.shape, 1)
        zero = jnp.zeros_like(qh)
        return jnp.concatenate([jnp.where(lane < ATT_QK_DIM, qh, zero),
                                jnp.where(lane >= ATT_QK_DIM, qh, zero)], axis=0)

    def run(key_blocks):
        def score_block(q2, h, b, buf, run_max):
            kr, _, r0 = key_blocks[b]
            s = _dot_nt(q2, kr[r0:r0 + KEY_BLOCK, h * LANE:(h + 1) * LANE])
            s_scr[buf, :, b * KEY_BLOCK:(b + 1) * KEY_BLOCK] = s
            blk_max = jnp.maximum(s[:, 0:LANE], s[:, LANE:KEY_BLOCK])
            return blk_max if run_max is None else jnp.maximum(run_max, blk_max)

        run_max = None
        q2 = stacked_queries(0)
        for b in range(len(key_blocks)):
            run_max = score_block(q2, 0, b, 0, run_max)
        for h in range(ATT_HEADS):
            sl = slice(h * LANE, (h + 1) * LANE)
            buf = h % 2
            m = jnp.max(run_max, axis=-1, keepdims=True)
            run_max, run_sum, acc = None, None, None
            if h + 1 < ATT_HEADS:
                q2 = stacked_queries(h + 1)
            for b, (_, vr, r0) in enumerate(key_blocks):
                if h + 1 < ATT_HEADS:
                    run_max = score_block(q2, h + 1, b, 1 - buf, run_max)
                p = jnp.exp2(s_scr[buf, :, b * KEY_BLOCK:(b + 1) * KEY_BLOCK] - m)
                blk_sum = p[:, 0:LANE] + p[:, LANE:KEY_BLOCK]
                run_sum = blk_sum if run_sum is None else run_sum + blk_sum
                part = _dot(p.astype(BF16), vr[r0:r0 + KEY_BLOCK, sl])
                acc = part if acc is None else acc + part
            o2 = acc / jnp.sum(run_sum, axis=-1, keepdims=True)
            oh = o2[0:TILE] - lam * o2[TILE:]
            oh = _rmsnorm_rows(oh, g_ref[0]) * (1.0 - lam_init)
            o_ref[:, sl] = oh.astype(BF16)

    @pl.when(i < NT_CTX)
    def _():
        run([(kc_ref, vc_ref, 0)])

    @pl.when(i >= NT_CTX)
    def _():
        run([(kp_ref.at[0, 0], vp_ref.at[0, 0], 0)]
            + [(kl_ref, vl_ref, r0) for r0 in range(0, DEC_SEQ, KEY_BLOCK)])


def _diff_attention(q, kb, vb, past_k, past_v, att_lambda, subln_g, lidx):
    lam_init = 0.8 - 0.6 * math.exp(-0.3 * lidx)
    ctx_idx = lambda i: (jnp.minimum(i, NT_CTX - 1), 0)
    lat_idx = lambda i: (N_CTX // DEC_SEQ + _lat_batch(i), 0)
    past_idx = lambda i: (_lat_batch(i), lidx, 0, 0)
    return pl.pallas_call(
        functools.partial(_attn_kernel, lam_init=lam_init),
        out_shape=jax.ShapeDtypeStruct((N_TOK, ATT_DIM), BF16),
        grid=(NT,),
        in_specs=[_row_spec(ATT_QK_TOTAL),
                  pl.BlockSpec((TILE, ATT_QK_TOTAL), ctx_idx),
                  pl.BlockSpec((TILE, ATT_DIM), ctx_idx),
                  pl.BlockSpec((DEC_SEQ, ATT_QK_TOTAL), lat_idx),
                  pl.BlockSpec((DEC_SEQ, ATT_DIM), lat_idx),
                  pl.BlockSpec((1, 1, PAST_LEN, ATT_QK_TOTAL), past_idx),
                  pl.BlockSpec((1, 1, PAST_LEN, ATT_DIM), past_idx),
                  _layer_spec((4, ATT_QK_DIM), lidx),
                  _layer_spec((1, ATT_V_DIM), lidx)],
        out_specs=_row_spec(ATT_DIM),
        scratch_shapes=[pltpu.VMEM((2, 2 * TILE, N_KEYS_LAT), F32)],
        compiler_params=_params(),
        name="diff_attention",
    )(q, kb, vb, kb, vb, past_k, past_v, att_lambda, subln_g)


def _ssd_scan_tile(direction, t, xc_scr, dt_ref, h0_ref, dtb_ref, asmall_ref, e_ref, st_ref,
                   state_scr, y_scr):
    is_ctx, _, j = _big_lat_index(t)
    seq_begins = (j == 0) if direction == 0 else (j == ROW_TILES_PER_LAT - 1)

    @pl.when(is_ctx)
    def _():
        state_scr[...] = jnp.zeros_like(state_scr)

    @pl.when(jnp.logical_and(jnp.logical_not(is_ctx), seq_begins))
    def _():
        state_scr[...] = h0_ref[0, 0, 0].T

    a_small = -jnp.exp(asmall_ref[...])
    row = lax.broadcasted_iota(jnp.int32, (SSD_CHUNK, SSD_CHUNK), 0)
    col = lax.broadcasted_iota(jnp.int32, (SSD_CHUNK, SSD_CHUNK), 1)
    keep = (col <= row) if direction == 0 else (col >= row)
    tri = jnp.where(keep, 1.0, 0.0).astype(BF16)
    pair_lane = lax.broadcasted_iota(jnp.int32, (SSD_CHUNK, LANE), 1)
    edge = SSD_CHUNK - 1 if direction == 0 else 0
    heads_per_group = SSD_HEADS // SSD_GROUPS

    chunks = range(ROW_TILE // SSD_CHUNK)
    rows = [slice(c * SSD_CHUNK, (c + 1) * SSD_CHUNK) for c in chunks]
    groups = [slice(g * SSD_GROUP_DIM, (g + 1) * SSD_GROUP_DIM) for g in range(SSD_GROUPS)]
    dts = [jax.nn.softplus(dt_ref[r, :] + dtb_ref[...]) for r in rows]
    dt_exp = [_dot_split(dt, e_ref[...], 2) for dt in dts]
    acs_s = [_dot_exact_lhs(tri, dt * a_small) for dt in dts]
    acs_exp = [_dot_split(a, e_ref[...], 3) for a in acs_s]
    acs_st = [a.T for a in acs_s]
    b_mat = [[xc_scr[r, SSD_DIM + g * SSD_STATE:SSD_DIM + (g + 1) * SSD_STATE]
              for g in range(SSD_GROUPS)] for r in rows]
    c_mat = [[xc_scr[r, SSD_DIM + (SSD_GROUPS + g) * SSD_STATE:
                     SSD_DIM + (SSD_GROUPS + g + 1) * SSD_STATE].astype(BF16)
              for g in range(SSD_GROUPS)] for r in rows]
    cb = [[_dot_nt(c_mat[c][g], b_mat[c][g].astype(BF16)) for g in range(SSD_GROUPS)] for c in chunks]
    xdt = [xc_scr[rows[c], 0:SSD_DIM] * dt_exp[c] for c in chunks]
    total = [a[edge:edge + 1, :] for a in acs_exp]
    xw = [(xdt[c] * jnp.exp(total[c] - acs_exp[c])).astype(BF16) for c in chunks]
    decay_from_entry = [jnp.exp(a) for a in acs_exp]
    y_diag = []
    for c in chunks:
        per_group = []
        for g in range(SSD_GROUPS):
            pairs = []
            for pr in range(heads_per_group // 2):
                ps = slice(g * SSD_GROUP_DIM + pr * LANE, g * SSD_GROUP_DIM + (pr + 1) * LANE)
                x_pair = xdt[c][:, ps].astype(BF16)
                halves = []
                for hh in range(2):
                    hcol = direction * SSD_HEADS + g * heads_per_group + pr * 2 + hh
                    diff = acs_s[c][:, hcol:hcol + 1] - acs_st[c][hcol:hcol + 1, :]
                    decay = jnp.where(keep, jnp.exp(diff), 0.0)
                    halves.append(_dot((cb[c][g] * decay).astype(BF16), x_pair))
                pairs.append(jnp.where(pair_lane < SSD_HEAD_DIM, halves[0], halves[1]))
            per_group.append(jnp.concatenate(pairs, axis=1))
        y_diag.append(per_group)
    state_in = [_dot(b_mat[c][g].T.astype(BF16), xw[c][:, groups[g]])
                for c in chunks for g in range(SSD_GROUPS)]

    h = [state_scr[:, gs] for gs in groups]
    chunks_per_seq = SEQ // SSD_CHUNK
    scan_order = list(chunks) if direction == 0 else list(reversed(chunks))
    for n, c in enumerate(scan_order):
        if n > 0 and n % chunks_per_seq == 0:
            h = [jnp.where(is_ctx, 0.0, hg) for hg in h]
        for g, gs in enumerate(groups):
            y_off = _dot(c_mat[c][g], h[g].astype(BF16)) * decay_from_entry[c][:, gs]
            y_scr[rows[c], gs] = y_diag[c][g] + y_off
            h[g] = h[g] * jnp.exp(total[c][:, gs]) + state_in[c * SSD_GROUPS + g]
        if (n + 1) % chunks_per_seq == 0:
            st_ref[c // chunks_per_seq, 0, 0] = jnp.concatenate(h, axis=1).T
    for g, gs in enumerate(groups):
        state_scr[:, gs] = h[g]


def _ssd_fwd_kernel(xbc_ref, xp_ref, xn_ref, dt_ref, h0_ref, cw_ref, cb_ref, dtb_ref, asmall_ref, e_ref,
                    st_in_ref, y_ref, xc_ref, st_ref, conv_scr, state_scr):
    del st_in_ref
    t = _lat_first_rows(pl.program_id(0))
    is_ctx, _, j = _big_lat_index(t)
    _fill_halves(conv_scr, xbc_ref, xp_ref[...], xn_ref[...], is_ctx, j)
    for hf in range(ROW_TILE // SEQ):
        b = hf * POOL_HALF_STRIDE + HALO
        conv = (cw_ref[0, 0:1, :] * conv_scr[b - 1:b - 1 + SEQ, :]
                + cw_ref[0, 1:2, :] * conv_scr[b:b + SEQ, :]
                + cw_ref[0, 2:3, :] * conv_scr[b + 1:b + 1 + SEQ, :] + cb_ref[0])
        xc_ref[hf * SEQ:(hf + 1) * SEQ, :] = _silu(conv)
    _ssd_scan_tile(0, t, xc_ref, dt_ref, h0_ref, dtb_ref, asmall_ref, e_ref, st_ref, state_scr, y_ref)


def _ssd_bwd_kernel(xc_ref, dt_ref, h0_ref, dtb_ref, asmall_ref, e_ref, yf_ref, z_ref, dskip_ref, ng_ref,
                    st_in_ref, o_ref, st_ref, state_scr, y_scr):
    del st_in_ref
    t = NRT - 1 - pl.program_id(0)
    _ssd_scan_tile(1, t, xc_ref, dt_ref, h0_ref, dtb_ref, asmall_ref, e_ref, st_ref, state_scr, y_scr)
    y = yf_ref[...] + y_scr[...] + dskip_ref[...] * xc_ref[:, 0:SSD_DIM]
    yz = y * _silu(z_ref[...])
    for g in range(SSD_GROUPS):
        gs = slice(g * SSD_GROUP_DIM, (g + 1) * SSD_GROUP_DIM)
        o_ref[:, gs] = _rmsnorm_rows(yz[:, gs], ng_ref[0, :, gs]).astype(BF16)


def _ssd_mixer(xbc, dt_raw, z, state_ssd, lidx, conv_w, conv_b, norm_g, p, new_state):
    st_shape = jax.ShapeDtypeStruct((BATCH, DEPTH, 2, SSD_DIM, SSD_STATE), F32)
    st_block = (ROW_TILE // SEQ, 1, 1, SSD_DIM, SSD_STATE)
    rev = lambda i: NRT - 1 - i
    fwd = _lat_first_rows
    state_scratch = pltpu.VMEM((SSD_STATE, SSD_DIM), F32)
    any_spec = pl.BlockSpec(memory_space=pl.ANY)

    def st_idx(order, direction, parked):
        def idx(i):
            t = order(i)
            return (jnp.where(t < NRT_CTX, t, parked), lidx, direction, 0, 0)
        return idx

    def scan_specs(order, direction):
        h0_idx = lambda i: (_big_lat_index(order(i))[1], lidx, direction, 0, 0)
        return [_big_row_spec(DT_PAD, order),
                pl.BlockSpec((1, 1, 1, SSD_DIM, SSD_STATE), h0_idx),
                _const_spec((1, DT_PAD)), _const_spec((1, DT_PAD)), _const_spec((DT_PAD, SSD_DIM))]

    def scan_args(direction):
        return [dt_raw, state_ssd, p["dt_bias"], p["alog_small"][direction], p["expand"][direction]]

    fwd_args = [xbc, xbc, xbc, *scan_args(0)[:2], conv_w, conv_b, *scan_args(0)[2:], new_state]
    y_f, xc, new_state = pl.pallas_call(
        _ssd_fwd_kernel,
        out_shape=[jax.ShapeDtypeStruct((N_TOK, SSD_DIM), F32),
                   jax.ShapeDtypeStruct((N_TOK, SSD_XBC_DIM), F32), st_shape],
        grid=(NRT,),
        in_specs=([_big_row_spec(SSD_XBC_DIM, fwd), _big_halo_spec(SSD_XBC_DIM, "prev", fwd),
                   _big_halo_spec(SSD_XBC_DIM, "next", fwd)]
                  + scan_specs(fwd, 0)[:2]
                  + [_layer_spec((3, SSD_XBC_DIM), lidx), _layer_spec((1, SSD_XBC_DIM), lidx)]
                  + scan_specs(fwd, 0)[2:] + [any_spec]),
        out_specs=[_big_row_spec(SSD_DIM, fwd), _big_row_spec(SSD_XBC_DIM, fwd),
                   pl.BlockSpec(st_block, st_idx(fwd, 0, 0))],
        scratch_shapes=[pltpu.VMEM((POOL_SCR_ROWS, SSD_XBC_DIM), F32), state_scratch],
        input_output_aliases={len(fwd_args) - 1: 2},
        compiler_params=_params(),
        name="ssd_forward",
    )(*fwd_args)

    bwd_args = [xc, *scan_args(1), y_f, z, p["d_skip"], norm_g, new_state]
    out, new_state = pl.pallas_call(
        _ssd_bwd_kernel,
        out_shape=[jax.ShapeDtypeStruct((N_TOK, SSD_DIM), BF16), st_shape],
        grid=(NRT,),
        in_specs=([_big_row_spec(SSD_XBC_DIM, rev)] + scan_specs(rev, 1)
                  + [_big_row_spec(SSD_DIM, rev), _big_row_spec(SSD_DIM, rev),
                     _const_spec((1, SSD_DIM)), _layer_spec((1, SSD_DIM), lidx), any_spec]),
        out_specs=[_big_row_spec(SSD_DIM, rev), pl.BlockSpec(st_block, st_idx(rev, 1, NRT_CTX - 1))],
        scratch_shapes=[state_scratch, pltpu.VMEM((ROW_TILE, SSD_DIM), F32)],
        input_output_aliases={len(bwd_args) - 1: 1},
        compiler_params=_params(),
        name="ssd_backward",
    )(*bwd_args)
    return out, new_state


def _merge_kernel(x_ref, mod_ref, g_ref, wg_ref, bg_ref, pooled_ref, pw_ref, ps_ref,
                  oa_ref, os_ref, wb_ref, wo_ref, o_ref):
    x = x_ref[...]
    hb = _modulated_norm(x, g_ref[0], mod_ref, 0).astype(BF16)
    wg = wg_ref.at[0]
    wb = wb_ref.at[0]

    def gated(n, branch, row0, width):
        cols = slice(n * D_MODEL, (n + 1) * D_MODEL)
        gate = jax.nn.sigmoid(_dot(hb, wg[:, cols]) + bg_ref[0, :, cols])
        return gate * _dot(branch, wb[row0:row0 + width, :])

    o_pool = jnp.concatenate(
        [_dot(pooled_ref[:, g * POOL_GROUP_DIM:(g + 1) * POOL_GROUP_DIM], pw_ref[0, g])
         for g in range(POOL_GROUPS)], axis=1) * ps_ref[0]
    merged = (gated(0, o_pool.astype(BF16), 0, POOL_DIM)
              + gated(1, oa_ref[...], POOL_DIM, ATT_DIM)
              + gated(2, os_ref[...], POOL_DIM + ATT_DIM, SSD_DIM))
    o_ref[...] = x + mod_ref[0, 2:3, :] * _dot(merged.astype(BF16), wo_ref[0])


def _merge(x, mod_l, norm_g, w_gate, b_gate, pooled, pool_w, pool_scale, o_att, o_ssd, w_branch, w_out,
           lidx):
    return pl.pallas_call(
        _merge_kernel,
        out_shape=jax.ShapeDtypeStruct((N_TOK, D_MODEL), F32),
        grid=(NRT,),
        in_specs=[_big_row_spec(D_MODEL), _big_mod_spec(),
                  _layer_spec((1, D_MODEL), lidx),
                  _layer_spec((D_MODEL, 3 * D_MODEL), lidx), _layer_spec((1, 3 * D_MODEL), lidx),
                  _big_row_spec(POOL_DIM),
                  _layer_spec((POOL_GROUPS, POOL_GROUP_DIM, POOL_GROUP_DIM), lidx),
                  _layer_spec((1, POOL_DIM), lidx),
                  _big_row_spec(ATT_DIM), _big_row_spec(SSD_DIM),
                  _layer_spec((MIX_DIM, D_MODEL), lidx), _layer_spec((D_MODEL, D_MODEL), lidx)],
        out_specs=_big_row_spec(D_MODEL),
        compiler_params=_params(),
        name="merge",
    )(x, mod_l, norm_g, w_gate, b_gate, pooled, pool_w, pool_scale, o_att, o_ssd, w_branch, w_out)


N_FFN_IN = 10


def _ffn_kernel(*refs, final):
    x_ref, xp_ref, xn_ref, mod_ref, g_ref, wu_ref, cw_ref, cb_ref, wd_ref, fg_ref = refs[:N_FFN_IN]
    h_scr, cols_scr = refs[-2:]
    out_refs = refs[N_FFN_IN:-2]
    res_ref = h_scr if final else out_refs[0]
    i = pl.program_id(0)
    is_ctx, _, j = _big_lat_index(i)
    first = jnp.logical_or(is_ctx, j == 0)
    last = jnp.logical_or(is_ctx, j == ROW_TILES_PER_LAT - 1)
    seq_len = jnp.where(is_ctx, SEQ, DEC_SEQ)
    tile_pos = jnp.where(is_ctx, 0, j * ROW_TILE)
    n_col = D_MODEL // LANE

    g = g_ref[0]
    h_nat = _modulated_norm(x_ref[...], g, mod_ref, 3)
    for cb in range(n_col):
        cols_scr[cb] = h_nat[:, cb * LANE:(cb + 1) * LANE]
    for r in range(ROW_GROUPS):
        for cb in range(n_col):
            h_scr[r * SUBLANE:(r + 1) * SUBLANE, cb * LANE:(cb + 1) * LANE] = (
                cols_scr[cb, pl.ds(r, SUBLANE, stride=ROW_GROUPS), :])
    h_scr[ROW_TILE:ROW_TILE + HALO, :] = jnp.where(first, 0.0, _modulated_norm(xp_ref[...], g, mod_ref, 3))
    h_scr[ROW_TILE + HALO:, :] = jnp.where(last, 0.0, _modulated_norm(xn_ref[...], g, mod_ref, 3))
    hb = h_scr[...].astype(BF16)

    sub = lax.broadcasted_iota(jnp.int32, (SUBLANE, 1), 0)
    pos_first = tile_pos + sub * ROW_GROUPS
    pos_last = pos_first + (ROW_GROUPS - 1)
    at_seq_start = (pos_first % seq_len) == 0
    at_seq_end = (pos_last % seq_len) == seq_len - 1
    wu = wu_ref.at[0]

    def up_proj(c):
        return (_dot(hb, wu[:, c * FFN_CHUNK:(c + 1) * FFN_CHUNK]),
                _dot(hb, wu[:, FFN_DIM + c * FFN_CHUNK:FFN_DIM + (c + 1) * FFN_CHUNK]))

    def conv_half(up, col0):
        cols = slice(col0, col0 + FFN_CHUNK)
        a = up[0:ROW_TILE]
        halo_prev = up[ROW_TILE + HALO - 1:ROW_TILE + HALO]
        halo_next = up[ROW_TILE + HALO:ROW_TILE + HALO + 1]
        wrap_prev = pltpu.roll(a[ROW_TILE - SUBLANE:ROW_TILE], 1, axis=0)
        wrap_prev = jnp.where(sub == 0, halo_prev, wrap_prev)
        wrap_prev = jnp.where(at_seq_start, 0.0, wrap_prev)
        wrap_next = pltpu.roll(a[0:SUBLANE], SUBLANE - 1, axis=0)
        wrap_next = jnp.where(sub == SUBLANE - 1, halo_next, wrap_next)
        wrap_next = jnp.where(at_seq_end, 0.0, wrap_next)
        before = jnp.concatenate([wrap_prev, a[0:ROW_TILE - SUBLANE]], axis=0)
        after = jnp.concatenate([a[SUBLANE:ROW_TILE], wrap_next], axis=0)
        return (cw_ref[0, 0:1, cols] * before + cw_ref[0, 1:2, cols] * a
                + cw_ref[0, 2:3, cols] * after + cb_ref[0, :, cols])

    acc = None
    n_chunks = FFN_DIM // FFN_CHUNK
    up_next = up_proj(0)
    for c in range(n_chunks):
        up_gate, up_val = up_next
        if c + 1 < n_chunks:
            up_next = up_proj(c + 1)
        a_gate = conv_half(up_gate, c * FFN_CHUNK)
        a_val = conv_half(up_val, FFN_DIM + c * FFN_CHUNK)
        act = (_silu(a_gate) * a_val).astype(BF16)
        part = _dot(act, wd_ref[0, c * FFN_CHUNK:(c + 1) * FFN_CHUNK, :])
        acc = part if acc is None else acc + part
    upd = mod_ref[0, 5:6, :] * acc
    for cb in range(n_col):
        cols_scr[cb] = upd[:, cb * LANE:(cb + 1) * LANE]
    groups_per_sub = ROW_GROUPS // SUBLANE
    for q in range(ROW_GROUPS):
        start = (q % groups_per_sub) * ROW_GROUPS + q // groups_per_sub
        rows = slice(q * SUBLANE, (q + 1) * SUBLANE)
        for cb in range(n_col):
            cs = slice(cb * LANE, (cb + 1) * LANE)
            res_ref[rows, cs] = x_ref[rows, cs] + cols_scr[cb, pl.ds(start, SUBLANE, stride=SUBLANE), :]
    if final:
        y = _rmsnorm_rows(res_ref[0:ROW_TILE, :], fg_ref[...])
        y_ctx_ref, y_lat_ref = out_refs

        @pl.when(is_ctx)
        def _():
            y_ctx_ref[...] = y

        @pl.when(jnp.logical_not(is_ctx))
        def _():
            y_lat_ref[...] = y


def _conv_ffn(x, mod_l, norm_g, w_up, conv_w, conv_b, w_down, final_g, lidx, final):
    if final:
        out_shape = [jax.ShapeDtypeStruct((N_CTX, D_MODEL), F32), jax.ShapeDtypeStruct((N_LAT, D_MODEL), F32)]
        out_specs = [pl.BlockSpec((ROW_TILE, D_MODEL), lambda i: (jnp.minimum(i, NRT_CTX - 1), 0)),
                     pl.BlockSpec((ROW_TILE, D_MODEL), lambda i: (jnp.maximum(i - NRT_CTX, 0), 0))]
    else:
        out_shape = jax.ShapeDtypeStruct((N_TOK, D_MODEL), F32)
        out_specs = _big_row_spec(D_MODEL)
    return pl.pallas_call(
        functools.partial(_ffn_kernel, final=final),
        out_shape=out_shape,
        grid=(NRT,),
        in_specs=[_big_row_spec(D_MODEL), _big_halo_spec(D_MODEL, "prev"), _big_halo_spec(D_MODEL, "next"),
                  _big_mod_spec(),
                  _layer_spec((1, D_MODEL), lidx),
                  _layer_spec((D_MODEL, 2 * FFN_DIM), lidx), _layer_spec((3, 2 * FFN_DIM), lidx),
                  _layer_spec((1, 2 * FFN_DIM), lidx), _layer_spec((FFN_DIM, D_MODEL), lidx),
                  _const_spec((1, D_MODEL))],
        out_specs=out_specs,
        scratch_shapes=[pltpu.VMEM((ROW_TILE + 2 * HALO, D_MODEL), F32),
                        pltpu.VMEM((D_MODEL // LANE, ROW_TILE, LANE), F32)],
        compiler_params=_params(),
        name="conv_ffn",
    )(x, x, x, mod_l, norm_g, w_up, conv_w, conv_b, w_down, final_g)


def _rope_tables():
    rows = DEC_SEQ // GRID_W
    row = jnp.repeat(jnp.arange(rows), GRID_W).astype(F32)
    col = jnp.tile(jnp.arange(GRID_W), rows).astype(F32)
    half = ATT_QK_DIM // 2
    inv = ROPE_BASE ** (-jnp.arange(0, half, 2, dtype=F32) / half)
    ar, ac = row[:, None] * inv, col[:, None] * inv
    cos = jnp.concatenate([jnp.cos(ar), jnp.cos(ar), jnp.cos(ac), jnp.cos(ac)], axis=-1)
    sin = jnp.concatenate([-jnp.sin(ar), jnp.sin(ar), -jnp.sin(ac), jnp.sin(ac)], axis=-1)
    reps = LANE // ATT_QK_DIM
    cos = jnp.concatenate([jnp.ones((ROW_TILE, LANE), F32), jnp.tile(cos, (1, reps))], axis=0)
    sin = jnp.concatenate([jnp.zeros((ROW_TILE, LANE), F32), jnp.tile(sin, (1, reps))], axis=0)
    return cos, sin


def _expand_matrices():
    e = np.zeros((2, DT_PAD, SSD_DIM), np.float32)
    for d in range(2):
        for h in range(SSD_HEADS):
            e[d, d * SSD_HEADS + h, h * SSD_HEAD_DIM:(h + 1) * SSD_HEAD_DIM] = 1.0
    return jnp.asarray(e, BF16)


def _ssd_small_params(l, ssd_dt_bias, ssd_A_log, ssd_D):
    n_dt = 2 * SSD_HEADS
    alog = ssd_A_log[l]
    alog_small = jnp.zeros((2, 1, DT_PAD), F32)
    for d in range(2):
        alog_small = alog_small.at[d, 0, d * SSD_HEADS:(d + 1) * SSD_HEADS].set(alog[d])
    return dict(
        dt_bias=jnp.pad(ssd_dt_bias[l].reshape(1, n_dt), ((0, 0), (0, DT_PAD - n_dt))),
        alog_small=alog_small,
        expand=_expand_matrices(),
        d_skip=jnp.repeat(ssd_D[l], SSD_HEAD_DIM)[None],
    )


def kernel(x_prompt, x_sample, c, cache_attn_k, cache_attn_v, state_ssd, c_ctx, w_mod, b_mod, norm1_g, w_in, pool_w, pool_scale, att_lambda, att_subln_g, ssd_conv_w, ssd_conv_b, ssd_dt_bias, ssd_A_log, ssd_D, ssd_norm_g, w_gate, b_gate, w_branch, w_out, norm2_g, ffn_w_up, ffn_conv_w, ffn_conv_b, ffn_w_down, final_norm_g):
    x = jnp.concatenate([x_prompt.reshape(N_CTX, D_MODEL), x_sample.reshape(N_LAT, D_MODEL)], axis=0)
    cvec = jnp.concatenate([c_ctx[None], c, jnp.zeros((MOD_ROWS - 1 - DEC_BATCH, D_MODEL), F32)], axis=0)
    mod = _modulation(cvec, w_mod, b_mod).reshape(DEPTH, MOD_ROWS, N_MOD, D_MODEL)
    rope_cos, rope_sin = _rope_tables()
    past_k = cache_attn_k.reshape(DEC_BATCH, DEPTH, PAST_LEN, ATT_QK_TOTAL).astype(BF16)
    past_v = cache_attn_v.reshape(DEC_BATCH, DEPTH, PAST_LEN, ATT_DIM).astype(BF16)
    past_state = state_ssd.reshape(DEC_BATCH, DEPTH, 2, SSD_DIM, SSD_STATE)

    n_dt = 2 * SSD_HEADS
    w_in_b = w_in.astype(BF16)
    w_dt_b = jnp.pad(w_in[:, :, _C_DT:], ((0, 0), (0, 0), (0, DT_PAD - n_dt))).astype(BF16)
    w_gate_b, w_branch_b, w_out_b = w_gate.astype(BF16), w_branch.astype(BF16), w_out.astype(BF16)
    w_up_b, w_down_b, pool_w_b = ffn_w_up.astype(BF16), ffn_w_down.astype(BF16), pool_w.astype(BF16)
    row = lambda a: a[:, None, :]
    norm1_r, norm2_r, b_gate_r, pool_scale_r = row(norm1_g), row(norm2_g), row(b_gate), row(pool_scale)
    subln_r, conv_b_r, ssd_norm_r, ffn_conv_b_r = row(att_subln_g), row(ssd_conv_b), row(ssd_norm_g), row(ffn_conv_b)
    final_g = final_norm_g[None]

    cache_shape = (BATCH, DEPTH, SEQ, ATT_QK_TOTAL)
    caches = (jnp.zeros(cache_shape, F32), jnp.zeros(cache_shape, F32))
    new_state = jnp.zeros((BATCH, DEPTH, 2, SSD_DIM, SSD_STATE), F32)
    for l in range(DEPTH):
        pooled, q, k_cache, v_cache, kb, vb, z, xbc, dt_raw = _inproj(x, mod[l], norm1_r, w_in_b, w_dt_b,
                                                                      rope_cos, rope_sin, l, caches)
        caches = (k_cache, v_cache)
        o_att = _diff_attention(q, kb, vb, past_k, past_v, att_lambda, subln_r, l)
        o_ssd, new_state = _ssd_mixer(xbc, dt_raw, z, past_state, l, ssd_conv_w, conv_b_r, ssd_norm_r,
                                      _ssd_small_params(l, ssd_dt_bias, ssd_A_log, ssd_D), new_state)
        x = _merge(x, mod[l], norm1_r, w_gate_b, b_gate_r, pooled, pool_w_b, pool_scale_r, o_att, o_ssd,
                   w_branch_b, w_out_b, l)
        x = _conv_ffn(x, mod[l], norm2_r, w_up_b, ffn_conv_w, ffn_conv_b_r, w_down_b, final_g, l,
                      final=(l == DEPTH - 1))

    y_ctx, y_lat = x
    return (y_ctx.reshape(BATCH, SEQ, D_MODEL), y_lat.reshape(DEC_BATCH, DEC_SEQ, D_MODEL),
            caches[0].reshape(BATCH, DEPTH, SEQ, ATT_HEADS, 2, ATT_QK_DIM),
            caches[1].reshape(BATCH, DEPTH, SEQ, ATT_HEADS, ATT_V_DIM),
            new_state.reshape(BATCH, DEPTH, 2, SSD_HEADS, SSD_HEAD_DIM, SSD_STATE))
```

```python
import functools
import math

import numpy as np
import jax
import jax.numpy as jnp
from jax import lax
from jax.experimental import pallas as pl
from jax.experimental.pallas import tpu as pltpu

D_MODEL = 1024
BATCH = 32
SEQ = 256
DEPTH = 4
DEC_BATCH = 4
DEC_SEQ = 2048
PAST_LEN = 256
GRID_W = 64
EPS = 1e-6
N_MOD = 6

POOL_GROUPS = 4
POOL_WINDOWS = (2, 4, 8, 16)
POOL_DIM = D_MODEL // 2
POOL_GROUP_DIM = POOL_DIM // POOL_GROUPS

ATT_HEADS = 4
ATT_QK_DIM = 64
ATT_V_DIM = 2 * ATT_QK_DIM
ATT_QK_TOTAL = ATT_HEADS * 2 * ATT_QK_DIM
ATT_DIM = ATT_HEADS * ATT_V_DIM
ROPE_BASE = 10000.0

SSD_DIM = D_MODEL // 2
SSD_HEAD_DIM = 64
SSD_HEADS = SSD_DIM // SSD_HEAD_DIM
SSD_GROUPS = 2
SSD_STATE = 128
SSD_CHUNK = 128
SSD_XBC_DIM = SSD_DIM + 2 * SSD_GROUPS * SSD_STATE
SSD_GROUP_DIM = SSD_DIM // SSD_GROUPS

MIX_DIM = POOL_DIM + ATT_DIM + SSD_DIM
FFN_DIM = ((8 * D_MODEL // 3 + 127) // 128) * 128

LANE = 128
SUBLANE = 8
HALO = SUBLANE

TILE = 256
N_CTX = BATCH * SEQ
N_LAT = DEC_BATCH * DEC_SEQ
N_TOK = N_CTX + N_LAT
NT_CTX = N_CTX // TILE
NT = N_TOK // TILE
TILES_PER_LAT = DEC_SEQ // TILE
N_HALO_BLOCKS = N_TOK // HALO
MOD_ROWS = 8
DT_PAD = LANE
MXU_WIDTH = 256
_FFN_SPLIT = (FFN_DIM // MXU_WIDTH + 1) // 2 * MXU_WIDTH
FFN_CHUNKS = ((0, _FFN_SPLIT), (_FFN_SPLIT, FFN_DIM))
VMEM_LIMIT = 56 * 1024 * 1024

ROW_TILE = 512
NRT = N_TOK // ROW_TILE
NRT_CTX = N_CTX // ROW_TILE
ROW_TILES_PER_LAT = DEC_SEQ // ROW_TILE
ROW_GROUPS = ROW_TILE // SUBLANE

KEY_BLOCK = 256
N_KEYS_LAT = PAST_LEN + DEC_SEQ

F32 = jnp.float32
BF16 = jnp.bfloat16

assert SEQ == TILE and DEC_SEQ % TILE == 0 and PAST_LEN == TILE
assert TILE == 2 * SSD_CHUNK and SSD_STATE == LANE and FFN_DIM % MXU_WIDTH == 0
assert ROW_TILE % SEQ == 0 and DEC_SEQ % ROW_TILE == 0
assert KEY_BLOCK == PAST_LEN == SEQ and DEC_SEQ % KEY_BLOCK == 0 and KEY_BLOCK == 2 * LANE


def _dot(a, b):
    return jnp.dot(a, b, preferred_element_type=F32)


def _dot_nt(a, b):
    return lax.dot_general(a, b, (((1,), (1,)), ((), ())), preferred_element_type=F32)


def _split3(x):
    x1 = x.astype(BF16)
    r1 = x - x1.astype(F32)
    x2 = r1.astype(BF16)
    x3 = (r1 - x2.astype(F32)).astype(BF16)
    return x1, x2, x3


def _dot_split(a, b_bf16, terms):
    acc = None
    for piece in _split3(a)[:terms]:
        part = _dot(piece, b_bf16)
        acc = part if acc is None else acc + part
    return acc


def _dot_exact_lhs(a_bf16, b):
    b1, b2, b3 = _split3(b)
    return _dot(a_bf16, b1) + _dot(a_bf16, b2) + _dot(a_bf16, b3)


def _silu(x):
    return x * jax.nn.sigmoid(x)


def _lat_batch(t):
    return jnp.maximum(t - NT_CTX, 0) // TILES_PER_LAT


def _const_spec(shape):
    nd = len(shape)
    return pl.BlockSpec(shape, lambda *_: (0,) * nd, pipeline_mode=pl.Buffered(1))


def _layer_spec(shape, lidx):
    nd = len(shape)
    return pl.BlockSpec((1,) + tuple(shape), lambda *_: (lidx,) + (0,) * nd,
                        pipeline_mode=pl.Buffered(1))


def _row_spec(cols):
    return pl.BlockSpec((TILE, cols), lambda i: (i, 0))


def _same_order(i):
    return i


def _lat_first_rows(i):
    return (i + NRT_CTX) % NRT


def _big_row_spec(cols, order=_same_order):
    return pl.BlockSpec((ROW_TILE, cols), lambda i: (order(i), 0))


def _big_halo_spec(cols, which, order=_same_order):
    per_tile = ROW_TILE // HALO
    if which == "prev":
        return pl.BlockSpec((HALO, cols), lambda i: (jnp.maximum(order(i) * per_tile - 1, 0), 0))
    return pl.BlockSpec((HALO, cols),
                        lambda i: (jnp.minimum((order(i) + 1) * per_tile, N_HALO_BLOCKS - 1), 0))


def _big_lat_index(t):
    rel = jnp.maximum(t - NRT_CTX, 0)
    return t < NRT_CTX, rel // ROW_TILES_PER_LAT, rel % ROW_TILES_PER_LAT


def _big_mod_spec(order=_same_order):
    def idx(i):
        is_ctx, b, _ = _big_lat_index(order(i))
        return (jnp.where(is_ctx, 0, 1 + b), 0, 0)
    return pl.BlockSpec((1, N_MOD, D_MODEL), idx)


def _params(n_axes=1):
    return pltpu.CompilerParams(dimension_semantics=("arbitrary",) * n_axes,
                                vmem_limit_bytes=VMEM_LIMIT)


def _rmsnorm_rows(x, g):
    ms = jnp.mean(x * x, axis=-1, keepdims=True)
    return x * lax.rsqrt(ms + EPS) * g


MOD_COLS = 1536


def _mod_kernel(c_ref, w_ref, b_ref, o_ref):
    s = _silu(c_ref[...]).astype(BF16)
    o_ref[0] = _dot(s, w_ref[0].astype(BF16)) + b_ref[0]


def _modulation(cvec, w_mod, b_mod):
    nmod = N_MOD * D_MODEL
    return pl.pallas_call(
        _mod_kernel,
        out_shape=jax.ShapeDtypeStruct((DEPTH, MOD_ROWS, nmod), F32),
        grid=(DEPTH, nmod // MOD_COLS),
        in_specs=[pl.BlockSpec((MOD_ROWS, D_MODEL), lambda l, j: (0, 0)),
                  pl.BlockSpec((1, D_MODEL, MOD_COLS), lambda l, j: (l, 0, j)),
                  pl.BlockSpec((1, 1, MOD_COLS), lambda l, j: (l, 0, j))],
        out_specs=pl.BlockSpec((1, MOD_ROWS, MOD_COLS), lambda l, j: (l, 0, j)),
        compiler_params=_params(2),
        name="modulation",
    )(cvec, w_mod, b_mod.reshape(DEPTH, 1, nmod))


_C_POOL = 0
_C_Q = _C_POOL + POOL_DIM
_C_K = _C_Q + ATT_QK_TOTAL
_C_V = _C_K + ATT_QK_TOTAL
_C_Z = _C_V + ATT_DIM
_C_XBC = _C_Z + SSD_DIM
_C_DT = _C_XBC + SSD_XBC_DIM
IN_DIM = _C_DT + 2 * SSD_HEADS
ROPE_SWAP = ATT_QK_DIM // 4
Q_PRESCALE = ATT_QK_DIM ** -0.5 * math.log2(math.e)


def _rope_block(xb, cos, sin):
    lane = lax.broadcasted_iota(jnp.int32, xb.shape, 1)
    first_half = (lane % (2 * ROPE_SWAP)) < ROPE_SWAP
    partner = jnp.where(first_half,
                        pltpu.roll(xb, LANE - ROPE_SWAP, axis=1),
                        pltpu.roll(xb, ROPE_SWAP, axis=1))
    return xb * cos + partner * sin


def _modulated_norm(x, g, mod_ref, shift_row):
    return (_rmsnorm_rows(x, g) * (1.0 + mod_ref[0, shift_row + 1:shift_row + 2, :])
            + mod_ref[0, shift_row:shift_row + 1, :])


def _inproj_kernel(x_ref, xp_ref, xn_ref, mod_ref, g_ref, w_ref, wdt_ref, cos_ref, sin_ref,
                   kc_in_ref, vc_in_ref,
                   pooled_ref, q_ref, kc_ref, vc_ref, kb_ref, vb_ref, z_ref, xbc_ref, dt_ref, pool_scr):
    del kc_in_ref, vc_in_ref
    is_ctx, _, j = _big_lat_index(_lat_first_rows(pl.program_id(0)))
    seqs = ROW_TILE // SEQ
    g = g_ref[0]
    hb = _modulated_norm(x_ref[...], g, mod_ref, 0).astype(BF16)
    w = w_ref.at[0]
    q = _dot(hb, w[:, _C_Q:_C_K])
    k = _dot(hb, w[:, _C_K:_C_V])
    cos = cos_ref[...]
    sin = sin_ref[...]
    for blk in range(ATT_QK_TOTAL // LANE):
        sl = slice(blk * LANE, (blk + 1) * LANE)
        q_ref[:, sl] = (_rope_block(q[:, sl], cos, sin) * Q_PRESCALE).astype(BF16)
        kr = _rope_block(k[:, sl], cos, sin)
        kc_ref[:, 0, :, sl] = kr.reshape(seqs, SEQ, LANE)
        kb_ref[:, sl] = kr.astype(BF16)
    h_halo = jnp.concatenate([_modulated_norm(xp_ref[...], g, mod_ref, 0),
                              _modulated_norm(xn_ref[...], g, mod_ref, 0)], axis=0).astype(BF16)
    u = _dot(jnp.concatenate([hb, h_halo], axis=0), w[:, _C_POOL:_C_Q])
    pooled_ref[...] = _pooled_deviation(u[0:ROW_TILE], u[ROW_TILE:ROW_TILE + HALO],
                                        u[ROW_TILE + HALO:], pool_scr, is_ctx, j)
    v = _dot(hb, w[:, _C_V:_C_Z])
    vc_ref[:, 0] = v.reshape(seqs, SEQ, ATT_DIM)
    vb_ref[...] = v.astype(BF16)
    z_ref[...] = _dot(hb, w[:, _C_Z:_C_XBC])
    dt_ref[...] = _dot(hb, wdt_ref[0])
    xbc_ref[...] = _dot(hb, w[:, _C_XBC:_C_DT])


def _inproj(x, mod_l, norm_g, w_in, w_dt, rope_cos, rope_sin, lidx, caches):
    order = _lat_first_rows
    seqs = ROW_TILE // SEQ
    cache_shape = (BATCH, DEPTH, SEQ, ATT_QK_TOTAL)
    row_outs = {0: (POOL_DIM, BF16), 1: (ATT_QK_TOTAL, BF16), 4: (ATT_QK_TOTAL, BF16), 5: (ATT_DIM, BF16),
                6: (SSD_DIM, F32), 7: (SSD_XBC_DIM, F32), 8: (DT_PAD, F32)}

    def cache_idx(i):
        t = order(i)
        return (jnp.where(t < NRT_CTX, t, 0), lidx, 0, 0)

    def rope_idx(i):
        is_ctx, _, j = _big_lat_index(order(i))
        return (jnp.where(is_ctx, 0, 1 + j), 0)

    out_shape, out_specs = [], []
    for n in range(9):
        if n in row_outs:
            c, d = row_outs[n]
            out_shape.append(jax.ShapeDtypeStruct((N_TOK, c), d))
            out_specs.append(_big_row_spec(c, order))
        else:
            out_shape.append(jax.ShapeDtypeStruct(cache_shape, F32))
            out_specs.append(pl.BlockSpec((seqs, 1, SEQ, ATT_QK_TOTAL), cache_idx))
    in_specs = [_big_row_spec(D_MODEL, order), _big_halo_spec(D_MODEL, "prev", order),
                _big_halo_spec(D_MODEL, "next", order), _big_mod_spec(order),
                _layer_spec((1, D_MODEL), lidx),
                _layer_spec((D_MODEL, IN_DIM), lidx),
                _layer_spec((D_MODEL, DT_PAD), lidx),
                pl.BlockSpec((ROW_TILE, LANE), rope_idx),
                pl.BlockSpec((ROW_TILE, LANE), rope_idx)]
    in_specs += [pl.BlockSpec(memory_space=pl.ANY)] * 2
    args = [x, x, x, mod_l, norm_g, w_in, w_dt, rope_cos, rope_sin, *caches]
    return pl.pallas_call(
        _inproj_kernel,
        out_shape=out_shape,
        grid=(NRT,),
        in_specs=in_specs,
        out_specs=out_specs,
        scratch_shapes=[pltpu.VMEM((POOL_SCR_ROWS, POOL_DIM), F32)],
        input_output_aliases={len(args) - 2: 2, len(args) - 1: 3},
        compiler_params=_params(),
        name="inproj",
    )(*args)


POOL_HALF_STRIDE = SEQ + 2 * HALO
POOL_SCR_ROWS = (ROW_TILE // SEQ) * POOL_HALF_STRIDE


def _fill_halves(scr, cur, prev, nxt, is_ctx, j):
    n_half = ROW_TILE // SEQ
    first = jnp.logical_or(is_ctx, j == 0)
    last = jnp.logical_or(is_ctx, j == ROW_TILES_PER_LAT - 1)
    for hf in range(n_half):
        base = hf * POOL_HALF_STRIDE
        if hf == 0:
            before = jnp.where(first, 0.0, prev)
        else:
            before = jnp.where(is_ctx, 0.0, cur[hf * SEQ - HALO:hf * SEQ, :])
        if hf == n_half - 1:
            after = jnp.where(last, 0.0, nxt)
        else:
            after = jnp.where(is_ctx, 0.0, cur[(hf + 1) * SEQ:(hf + 1) * SEQ + HALO, :])
        scr[base:base + HALO, :] = before
        scr[base + HALO:base + HALO + SEQ, :] = cur[hf * SEQ:(hf + 1) * SEQ, :]
        scr[base + HALO + SEQ:base + POOL_HALF_STRIDE, :] = after


def _pooled_deviation(u, u_prev, u_next, scr, is_ctx, j):
    n_half = ROW_TILE // SEQ
    seq_len = jnp.where(is_ctx, SEQ, DEC_SEQ)
    _fill_halves(scr, u, u_prev, u_next, is_ctx, j)
    halves = []
    for hf in range(n_half):
        base = hf * POOL_HALF_STRIDE + HALO
        pos = (jnp.where(is_ctx, 0, j * ROW_TILE + hf * SEQ)
               + lax.broadcasted_iota(jnp.int32, (SEQ, 1), 0))
        groups = []
        for g, win in enumerate(POOL_WINDOWS):
            half = win // 2
            sl = slice(g * POOL_GROUP_DIM, (g + 1) * POOL_GROUP_DIM)
            acc = scr[base - half:base - half + SEQ, sl]
            for k in range(1 - half, half):
                acc = acc + scr[base + k:base + k + SEQ, sl]
            lo = jnp.maximum(pos - half, 0)
            hi = jnp.minimum(pos + half - 1, seq_len - 1)
            inv_count = 1.0 / (hi - lo + 1).astype(F32)
            groups.append((acc * inv_count - scr[base:base + SEQ, sl]).astype(BF16))
        halves.append(jnp.concatenate(groups, axis=1))
    return jnp.concatenate(halves, axis=0)


def _attn_kernel(q_ref, kc_ref, vc_ref, kl_ref, vl_ref, kp_ref, vp_ref, lam_ref, g_ref, o_ref,
                 s_scr, *, lam_init):
    i = pl.program_id(0)
    lp = lam_ref[0]
    lam = (jnp.exp(jnp.sum(lp[0:1] * lp[1:2], axis=-1, keepdims=True))
           - jnp.exp(jnp.sum(lp[2:3] * lp[3:4], axis=-1, keepdims=True)) + lam_init)

    def stacked_queries(h):
        qh = q_ref[:, h * LANE:(h + 1) * LANE]
        lane = lax.broadcasted_iota(jnp.int32, qh.shape, 1)
        zero = jnp.zeros_like(qh)
        return jnp.concatenate([jnp.where(lane < ATT_QK_DIM, qh, zero),
                                jnp.where(lane >= ATT_QK_DIM, qh, zero)], axis=0)

    def run(key_blocks):
        def score_block(q2, h, b, buf, run_max):
            kr, _, r0 = key_blocks[b]
            s = _dot_nt(q2, kr[r0:r0 + KEY_BLOCK, h * LANE:(h + 1) * LANE])
            s_scr[buf, :, b * KEY_BLOCK:(b + 1) * KEY_BLOCK] = s
            blk_max = jnp.maximum(s[:, 0:LANE], s[:, LANE:KEY_BLOCK])
            return blk_max if run_max is None else jnp.maximum(run_max, blk_max)

        run_max = None
        q2 = stacked_queries(0)
        for b in range(len(key_blocks)):
            run_max = score_block(q2, 0, b, 0, run_max)
        for h in range(ATT_HEADS):
            sl = slice(h * LANE, (h + 1) * LANE)
            buf = h % 2
            m = jnp.max(run_max, axis=-1, keepdims=True)
            run_max, run_sum, acc = None, None, None
            if h + 1 < ATT_HEADS:
                q2 = stacked_queries(h + 1)
            for b, (_, vr, r0) in enumerate(key_blocks):
                if h + 1 < ATT_HEADS:
                    run_max = score_block(q2, h + 1, b, 1 - buf, run_max)
                p = jnp.exp2(s_scr[buf, :, b * KEY_BLOCK:(b + 1) * KEY_BLOCK] - m)
                blk_sum = p[:, 0:LANE] + p[:, LANE:KEY_BLOCK]
                run_sum = blk_sum if run_sum is None else run_sum + blk_sum
                part = _dot(p.astype(BF16), vr[r0:r0 + KEY_BLOCK, sl])
                acc = part if acc is None else acc + part
            o2 = acc / jnp.sum(run_sum, axis=-1, keepdims=True)
            oh = o2[0:TILE] - lam * o2[TILE:]
            oh = _rmsnorm_rows(oh, g_ref[0]) * (1.0 - lam_init)
            o_ref[:, sl] = oh.astype(BF16)

    @pl.when(i < NT_CTX)
    def _():
        run([(kc_ref, vc_ref, 0)])

    @pl.when(i >= NT_CTX)
    def _():
        run([(kp_ref.at[0, 0], vp_ref.at[0, 0], 0)]
            + [(kl_ref, vl_ref, r0) for r0 in range(0, DEC_SEQ, KEY_BLOCK)])


def _diff_attention(q, kb, vb, past_k, past_v, att_lambda, subln_g, lidx):
    lam_init = 0.8 - 0.6 * math.exp(-0.3 * lidx)
    ctx_idx = lambda i: (jnp.minimum(i, NT_CTX - 1), 0)
    lat_idx = lambda i: (N_CTX // DEC_SEQ + _lat_batch(i), 0)
    past_idx = lambda i: (_lat_batch(i), lidx, 0, 0)
    return pl.pallas_call(
        functools.partial(_attn_kernel, lam_init=lam_init),
        out_shape=jax.ShapeDtypeStruct((N_TOK, ATT_DIM), BF16),
        grid=(NT,),
        in_specs=[_row_spec(ATT_QK_TOTAL),
                  pl.BlockSpec((TILE, ATT_QK_TOTAL), ctx_idx),
                  pl.BlockSpec((TILE, ATT_DIM), ctx_idx),
                  pl.BlockSpec((DEC_SEQ, ATT_QK_TOTAL), lat_idx),
                  pl.BlockSpec((DEC_SEQ, ATT_DIM), lat_idx),
                  pl.BlockSpec((1, 1, PAST_LEN, ATT_QK_TOTAL), past_idx),
                  pl.BlockSpec((1, 1, PAST_LEN, ATT_DIM), past_idx),
                  _layer_spec((4, ATT_QK_DIM), lidx),
                  _layer_spec((1, ATT_V_DIM), lidx)],
        out_specs=_row_spec(ATT_DIM),
        scratch_shapes=[pltpu.VMEM((2, 2 * TILE, N_KEYS_LAT), F32)],
        compiler_params=_params(),
        name="diff_attention",
    )(q, kb, vb, kb, vb, past_k, past_v, att_lambda, subln_g)


def _ssd_scan_tile(direction, t, xc_scr, dt_ref, h0_ref, dtb_ref, asmall_ref, e_ref, st_ref,
                   state_scr, y_scr):
    is_ctx, _, j = _big_lat_index(t)
    seq_begins = (j == 0) if direction == 0 else (j == ROW_TILES_PER_LAT - 1)

    @pl.when(is_ctx)
    def _():
        state_scr[...] = jnp.zeros_like(state_scr)

    @pl.when(jnp.logical_and(jnp.logical_not(is_ctx), seq_begins))
    def _():
        state_scr[...] = h0_ref[0, 0, 0].T

    a_small = -jnp.exp(asmall_ref[...])
    row = lax.broadcasted_iota(jnp.int32, (SSD_CHUNK, SSD_CHUNK), 0)
    col = lax.broadcasted_iota(jnp.int32, (SSD_CHUNK, SSD_CHUNK), 1)
    keep = (col <= row) if direction == 0 else (col >= row)
    tri = jnp.where(keep, 1.0, 0.0).astype(BF16)
    pair_lane = lax.broadcasted_iota(jnp.int32, (SSD_CHUNK, LANE), 1)
    edge = SSD_CHUNK - 1 if direction == 0 else 0
    heads_per_group = SSD_HEADS // SSD_GROUPS

    chunks = range(ROW_TILE // SSD_CHUNK)
    rows = [slice(c * SSD_CHUNK, (c + 1) * SSD_CHUNK) for c in chunks]
    groups = [slice(g * SSD_GROUP_DIM, (g + 1) * SSD_GROUP_DIM) for g in range(SSD_GROUPS)]
    dts = [jax.nn.softplus(dt_ref[r, :] + dtb_ref[...]) for r in rows]
    dt_exp = [_dot_split(dt, e_ref[...], 2) for dt in dts]
    acs_s = [_dot_exact_lhs(tri, dt * a_small) for dt in dts]
    acs_exp = [_dot_split(a, e_ref[...], 3) for a in acs_s]
    acs_st = [a.T for a in acs_s]
    b_mat = [[xc_scr[r, SSD_DIM + g * SSD_STATE:SSD_DIM + (g + 1) * SSD_STATE]
              for g in range(SSD_GROUPS)] for r in rows]
    c_mat = [[xc_scr[r, SSD_DIM + (SSD_GROUPS + g) * SSD_STATE:
                     SSD_DIM + (SSD_GROUPS + g + 1) * SSD_STATE].astype(BF16)
              for g in range(SSD_GROUPS)] for r in rows]
    cb = [[_dot_nt(c_mat[c][g], b_mat[c][g].astype(BF16)) for g in range(SSD_GROUPS)] for c in chunks]
    xdt = [xc_scr[rows[c], 0:SSD_DIM] * dt_exp[c] for c in chunks]
    total = [a[edge:edge + 1, :] for a in acs_exp]
    xw = [(xdt[c] * jnp.exp(total[c] - acs_exp[c])).astype(BF16) for c in chunks]
    decay_from_entry = [jnp.exp(a) for a in acs_exp]
    y_diag = []
    for c in chunks:
        per_group = []
        for g in range(SSD_GROUPS):
            pairs = []
            for pr in range(heads_per_group // 2):
                ps = slice(g * SSD_GROUP_DIM + pr * LANE, g * SSD_GROUP_DIM + (pr + 1) * LANE)
                x_pair = xdt[c][:, ps].astype(BF16)
                halves = []
                for hh in range(2):
                    hcol = direction * SSD_HEADS + g * heads_per_group + pr * 2 + hh
                    diff = acs_s[c][:, hcol:hcol + 1] - acs_st[c][hcol:hcol + 1, :]
                    decay = jnp.where(keep, jnp.exp(diff), 0.0)
                    halves.append(_dot((cb[c][g] * decay).astype(BF16), x_pair))
                pairs.append(jnp.where(pair_lane < SSD_HEAD_DIM, halves[0], halves[1]))
            per_group.append(jnp.concatenate(pairs, axis=1))
        y_diag.append(per_group)
    state_in = [_dot(b_mat[c][g].T.astype(BF16), xw[c][:, groups[g]])
                for c in chunks for g in range(SSD_GROUPS)]

    h = [state_scr[:, gs] for gs in groups]
    chunks_per_seq = SEQ // SSD_CHUNK
    scan_order = list(chunks) if direction == 0 else list(reversed(chunks))
    for n, c in enumerate(scan_order):
        if n > 0 and n % chunks_per_seq == 0:
            h = [jnp.where(is_ctx, 0.0, hg) for hg in h]
        for g, gs in enumerate(groups):
            y_off = _dot(c_mat[c][g], h[g].astype(BF16)) * decay_from_entry[c][:, gs]
            y_scr[rows[c], gs] = y_diag[c][g] + y_off
            h[g] = h[g] * jnp.exp(total[c][:, gs]) + state_in[c * SSD_GROUPS + g]
        if (n + 1) % chunks_per_seq == 0:
            st_ref[c // chunks_per_seq, 0, 0] = jnp.concatenate(h, axis=1).T
    for g, gs in enumerate(groups):
        state_scr[:, gs] = h[g]


def _ssd_fwd_kernel(xbc_ref, xp_ref, xn_ref, dt_ref, h0_ref, cw_ref, cb_ref, dtb_ref, asmall_ref, e_ref,
                    st_in_ref, y_ref, xc_ref, st_ref, conv_scr, state_scr):
    del st_in_ref
    t = _lat_first_rows(pl.program_id(0))
    is_ctx, _, j = _big_lat_index(t)
    _fill_halves(conv_scr, xbc_ref, xp_ref[...], xn_ref[...], is_ctx, j)
    for hf in range(ROW_TILE // SEQ):
        b = hf * POOL_HALF_STRIDE + HALO
        conv = (cw_ref[0, 0:1, :] * conv_scr[b - 1:b - 1 + SEQ, :]
                + cw_ref[0, 1:2, :] * conv_scr[b:b + SEQ, :]
                + cw_ref[0, 2:3, :] * conv_scr[b + 1:b + 1 + SEQ, :] + cb_ref[0])
        xc_ref[hf * SEQ:(hf + 1) * SEQ, :] = _silu(conv)
    _ssd_scan_tile(0, t, xc_ref, dt_ref, h0_ref, dtb_ref, asmall_ref, e_ref, st_ref, state_scr, y_ref)


def _ssd_bwd_kernel(xc_ref, dt_ref, h0_ref, dtb_ref, asmall_ref, e_ref, yf_ref, z_ref, dskip_ref, ng_ref,
                    st_in_ref, o_ref, st_ref, state_scr, y_scr):
    del st_in_ref
    t = NRT - 1 - pl.program_id(0)
    _ssd_scan_tile(1, t, xc_ref, dt_ref, h0_ref, dtb_ref, asmall_ref, e_ref, st_ref, state_scr, y_scr)
    y = yf_ref[...] + y_scr[...] + dskip_ref[...] * xc_ref[:, 0:SSD_DIM]
    yz = y * _silu(z_ref[...])
    for g in range(SSD_GROUPS):
        gs = slice(g * SSD_GROUP_DIM, (g + 1) * SSD_GROUP_DIM)
        o_ref[:, gs] = _rmsnorm_rows(yz[:, gs], ng_ref[0, :, gs]).astype(BF16)


def _ssd_mixer(xbc, dt_raw, z, state_ssd, lidx, conv_w, conv_b, norm_g, p, new_state):
    st_shape = jax.ShapeDtypeStruct((BATCH, DEPTH, 2, SSD_DIM, SSD_STATE), F32)
    st_block = (ROW_TILE // SEQ, 1, 1, SSD_DIM, SSD_STATE)
    rev = lambda i: NRT - 1 - i
    fwd = _lat_first_rows
    state_scratch = pltpu.VMEM((SSD_STATE, SSD_DIM), F32)
    any_spec = pl.BlockSpec(memory_space=pl.ANY)

    def st_idx(order, direction, parked):
        def idx(i):
            t = order(i)
            return (jnp.where(t < NRT_CTX, t, parked), lidx, direction, 0, 0)
        return idx

    def scan_specs(order, direction):
        h0_idx = lambda i: (_big_lat_index(order(i))[1], lidx, direction, 0, 0)
        return [_big_row_spec(DT_PAD, order),
                pl.BlockSpec((1, 1, 1, SSD_DIM, SSD_STATE), h0_idx),
                _const_spec((1, DT_PAD)), _const_spec((1, DT_PAD)), _const_spec((DT_PAD, SSD_DIM))]

    def scan_args(direction):
        return [dt_raw, state_ssd, p["dt_bias"], p["alog_small"][direction], p["expand"][direction]]

    fwd_args = [xbc, xbc, xbc, *scan_args(0)[:2], conv_w, conv_b, *scan_args(0)[2:], new_state]
    y_f, xc, new_state = pl.pallas_call(
        _ssd_fwd_kernel,
        out_shape=[jax.ShapeDtypeStruct((N_TOK, SSD_DIM), F32),
                   jax.ShapeDtypeStruct((N_TOK, SSD_XBC_DIM), F32), st_shape],
        grid=(NRT,),
        in_specs=([_big_row_spec(SSD_XBC_DIM, fwd), _big_halo_spec(SSD_XBC_DIM, "prev", fwd),
                   _big_halo_spec(SSD_XBC_DIM, "next", fwd)]
                  + scan_specs(fwd, 0)[:2]
                  + [_layer_spec((3, SSD_XBC_DIM), lidx), _layer_spec((1, SSD_XBC_DIM), lidx)]
                  + scan_specs(fwd, 0)[2:] + [any_spec]),
        out_specs=[_big_row_spec(SSD_DIM, fwd), _big_row_spec(SSD_XBC_DIM, fwd),
                   pl.BlockSpec(st_block, st_idx(fwd, 0, 0))],
        scratch_shapes=[pltpu.VMEM((POOL_SCR_ROWS, SSD_XBC_DIM), F32), state_scratch],
        input_output_aliases={len(fwd_args) - 1: 2},
        compiler_params=_params(),
        name="ssd_forward",
    )(*fwd_args)

    bwd_args = [xc, *scan_args(1), y_f, z, p["d_skip"], norm_g, new_state]
    out, new_state = pl.pallas_call(
        _ssd_bwd_kernel,
        out_shape=[jax.ShapeDtypeStruct((N_TOK, SSD_DIM), BF16), st_shape],
        grid=(NRT,),
        in_specs=([_big_row_spec(SSD_XBC_DIM, rev)] + scan_specs(rev, 1)
                  + [_big_row_spec(SSD_DIM, rev), _big_row_spec(SSD_DIM, rev),
                     _const_spec((1, SSD_DIM)), _layer_spec((1, SSD_DIM), lidx), any_spec]),
        out_specs=[_big_row_spec(SSD_DIM, rev), pl.BlockSpec(st_block, st_idx(rev, 1, NRT_CTX - 1))],
        scratch_shapes=[state_scratch, pltpu.VMEM((ROW_TILE, SSD_DIM), F32)],
        input_output_aliases={len(bwd_args) - 1: 1},
        compiler_params=_params(),
        name="ssd_backward",
    )(*bwd_args)
    return out, new_state


def _merge_kernel(x_ref, mod_ref, g_ref, wg_ref, bg_ref, pooled_ref, pw_ref, ps_ref,
                  oa_ref, os_ref, wb_ref, wo_ref, o_ref):
    x = x_ref[...]
    hb = _modulated_norm(x, g_ref[0], mod_ref, 0).astype(BF16)
    wg = wg_ref.at[0]
    wb = wb_ref.at[0]

    def gated(n, branch, row0, width):
        cols = slice(n * D_MODEL, (n + 1) * D_MODEL)
        gate = jax.nn.sigmoid(_dot(hb, wg[:, cols]) + bg_ref[0, :, cols])
        return gate * _dot(branch, wb[row0:row0 + width, :])

    o_pool = jnp.concatenate(
        [_dot(pooled_ref[:, g * POOL_GROUP_DIM:(g + 1) * POOL_GROUP_DIM], pw_ref[0, g])
         for g in range(POOL_GROUPS)], axis=1) * ps_ref[0]
    merged = (gated(0, o_pool.astype(BF16), 0, POOL_DIM)
              + gated(1, oa_ref[...], POOL_DIM, ATT_DIM)
              + gated(2, os_ref[...], POOL_DIM + ATT_DIM, SSD_DIM))
    o_ref[...] = x + mod_ref[0, 2:3, :] * _dot(merged.astype(BF16), wo_ref[0])


def _merge(x, mod_l, norm_g, w_gate, b_gate, pooled, pool_w, pool_scale, o_att, o_ssd, w_branch, w_out,
           lidx):
    return pl.pallas_call(
        _merge_kernel,
        out_shape=jax.ShapeDtypeStruct((N_TOK, D_MODEL), F32),
        grid=(NRT,),
        in_specs=[_big_row_spec(D_MODEL), _big_mod_spec(),
                  _layer_spec((1, D_MODEL), lidx),
                  _layer_spec((D_MODEL, 3 * D_MODEL), lidx), _layer_spec((1, 3 * D_MODEL), lidx),
                  _big_row_spec(POOL_DIM),
                  _layer_spec((POOL_GROUPS, POOL_GROUP_DIM, POOL_GROUP_DIM), lidx),
                  _layer_spec((1, POOL_DIM), lidx),
                  _big_row_spec(ATT_DIM), _big_row_spec(SSD_DIM),
                  _layer_spec((MIX_DIM, D_MODEL), lidx), _layer_spec((D_MODEL, D_MODEL), lidx)],
        out_specs=_big_row_spec(D_MODEL),
        compiler_params=_params(),
        name="merge",
    )(x, mod_l, norm_g, w_gate, b_gate, pooled, pool_w, pool_scale, o_att, o_ssd, w_branch, w_out)


N_FFN_IN = 10


def _ffn_kernel(*refs, final):
    x_ref, xp_ref, xn_ref, mod_ref, g_ref, wu_ref, cw_ref, cb_ref, wd_ref, fg_ref = refs[:N_FFN_IN]
    h_scr, cols_scr = refs[-2:]
    out_refs = refs[N_FFN_IN:-2]
    res_ref = h_scr if final else out_refs[0]
    i = pl.program_id(0)
    is_ctx, _, j = _big_lat_index(i)
    first = jnp.logical_or(is_ctx, j == 0)
    last = jnp.logical_or(is_ctx, j == ROW_TILES_PER_LAT - 1)
    seq_len = jnp.where(is_ctx, SEQ, DEC_SEQ)
    tile_pos = jnp.where(is_ctx, 0, j * ROW_TILE)
    n_col = D_MODEL // LANE

    g = g_ref[0]
    h_nat = _modulated_norm(x_ref[...], g, mod_ref, 3)
    for cb in range(n_col):
        cols_scr[cb] = h_nat[:, cb * LANE:(cb + 1) * LANE]
    for r in range(ROW_GROUPS):
        for cb in range(n_col):
            h_scr[r * SUBLANE:(r + 1) * SUBLANE, cb * LANE:(cb + 1) * LANE] = (
                cols_scr[cb, pl.ds(r, SUBLANE, stride=ROW_GROUPS), :])
    h_scr[ROW_TILE:ROW_TILE + HALO, :] = jnp.where(first, 0.0, _modulated_norm(xp_ref[...], g, mod_ref, 3))
    h_scr[ROW_TILE + HALO:, :] = jnp.where(last, 0.0, _modulated_norm(xn_ref[...], g, mod_ref, 3))
    hb = h_scr[...].astype(BF16)

    sub = lax.broadcasted_iota(jnp.int32, (SUBLANE, 1), 0)
    pos_first = tile_pos + sub * ROW_GROUPS
    pos_last = pos_first + (ROW_GROUPS - 1)
    at_seq_start = (pos_first % seq_len) == 0
    at_seq_end = (pos_last % seq_len) == seq_len - 1
    wu = wu_ref.at[0]

    def up_proj(c):
        lo, hi = FFN_CHUNKS[c]
        return (_dot(hb, wu[:, lo:hi]), _dot(hb, wu[:, FFN_DIM + lo:FFN_DIM + hi]))

    def conv_half(up, col0):
        cols = slice(col0, col0 + up.shape[1])
        a = up[0:ROW_TILE]
        halo_prev = up[ROW_TILE + HALO - 1:ROW_TILE + HALO]
        halo_next = up[ROW_TILE + HALO:ROW_TILE + HALO + 1]
        wrap_prev = pltpu.roll(a[ROW_TILE - SUBLANE:ROW_TILE], 1, axis=0)
        wrap_prev = jnp.where(sub == 0, halo_prev, wrap_prev)
        wrap_prev = jnp.where(at_seq_start, 0.0, wrap_prev)
        wrap_next = pltpu.roll(a[0:SUBLANE], SUBLANE - 1, axis=0)
        wrap_next = jnp.where(sub == SUBLANE - 1, halo_next, wrap_next)
        wrap_next = jnp.where(at_seq_end, 0.0, wrap_next)
        before = jnp.concatenate([wrap_prev, a[0:ROW_TILE - SUBLANE]], axis=0)
        after = jnp.concatenate([a[SUBLANE:ROW_TILE], wrap_next], axis=0)
        return (cw_ref[0, 0:1, cols] * before + cw_ref[0, 1:2, cols] * a
                + cw_ref[0, 2:3, cols] * after + cb_ref[0, :, cols])

    acc = None
    n_chunks = len(FFN_CHUNKS)
    up_next = up_proj(0)
    for c, (lo, hi) in enumerate(FFN_CHUNKS):
        up_gate, up_val = up_next
        if c + 1 < n_chunks:
            up_next = up_proj(c + 1)
        a_gate = conv_half(up_gate, lo)
        a_val = conv_half(up_val, FFN_DIM + lo)
        act = (_silu(a_gate) * a_val).astype(BF16)
        part = _dot(act, wd_ref[0, lo:hi, :])
        acc = part if acc is None else acc + part
    upd = mod_ref[0, 5:6, :] * acc
    for cb in range(n_col):
        cols_scr[cb] = upd[:, cb * LANE:(cb + 1) * LANE]
    groups_per_sub = ROW_GROUPS // SUBLANE
    for q in range(ROW_GROUPS):
        start = (q % groups_per_sub) * ROW_GROUPS + q // groups_per_sub
        rows = slice(q * SUBLANE, (q + 1) * SUBLANE)
        for cb in range(n_col):
            cs = slice(cb * LANE, (cb + 1) * LANE)
            res_ref[rows, cs] = x_ref[rows, cs] + cols_scr[cb, pl.ds(start, SUBLANE, stride=SUBLANE), :]
    if final:
        y = _rmsnorm_rows(res_ref[0:ROW_TILE, :], fg_ref[...])
        y_ctx_ref, y_lat_ref = out_refs

        @pl.when(is_ctx)
        def _():
            y_ctx_ref[...] = y

        @pl.when(jnp.logical_not(is_ctx))
        def _():
            y_lat_ref[...] = y


def _conv_ffn(x, mod_l, norm_g, w_up, conv_w, conv_b, w_down, final_g, lidx, final):
    if final:
        out_shape = [jax.ShapeDtypeStruct((N_CTX, D_MODEL), F32), jax.ShapeDtypeStruct((N_LAT, D_MODEL), F32)]
        out_specs = [pl.BlockSpec((ROW_TILE, D_MODEL), lambda i: (jnp.minimum(i, NRT_CTX - 1), 0)),
                     pl.BlockSpec((ROW_TILE, D_MODEL), lambda i: (jnp.maximum(i - NRT_CTX, 0), 0))]
    else:
        out_shape = jax.ShapeDtypeStruct((N_TOK, D_MODEL), F32)
        out_specs = _big_row_spec(D_MODEL)
    return pl.pallas_call(
        functools.partial(_ffn_kernel, final=final),
        out_shape=out_shape,
        grid=(NRT,),
        in_specs=[_big_row_spec(D_MODEL), _big_halo_spec(D_MODEL, "prev"), _big_halo_spec(D_MODEL, "next"),
                  _big_mod_spec(),
                  _layer_spec((1, D_MODEL), lidx),
                  _layer_spec((D_MODEL, 2 * FFN_DIM), lidx), _layer_spec((3, 2 * FFN_DIM), lidx),
                  _layer_spec((1, 2 * FFN_DIM), lidx), _layer_spec((FFN_DIM, D_MODEL), lidx),
                  _const_spec((1, D_MODEL))],
        out_specs=out_specs,
        scratch_shapes=[pltpu.VMEM((ROW_TILE + 2 * HALO, D_MODEL), F32),
                        pltpu.VMEM((D_MODEL // LANE, ROW_TILE, LANE), F32)],
        compiler_params=_params(),
        name="conv_ffn",
    )(x, x, x, mod_l, norm_g, w_up, conv_w, conv_b, w_down, final_g)


def _rope_tables():
    rows = DEC_SEQ // GRID_W
    row = jnp.repeat(jnp.arange(rows), GRID_W).astype(F32)
    col = jnp.tile(jnp.arange(GRID_W), rows).astype(F32)
    half = ATT_QK_DIM // 2
    inv = ROPE_BASE ** (-jnp.arange(0, half, 2, dtype=F32) / half)
    ar, ac = row[:, None] * inv, col[:, None] * inv
    cos = jnp.concatenate([jnp.cos(ar), jnp.cos(ar), jnp.cos(ac), jnp.cos(ac)], axis=-1)
    sin = jnp.concatenate([-jnp.sin(ar), jnp.sin(ar), -jnp.sin(ac), jnp.sin(ac)], axis=-1)
    reps = LANE // ATT_QK_DIM
    cos = jnp.concatenate([jnp.ones((ROW_TILE, LANE), F32), jnp.tile(cos, (1, reps))], axis=0)
    sin = jnp.concatenate([jnp.zeros((ROW_TILE, LANE), F32), jnp.tile(sin, (1, reps))], axis=0)
    return cos, sin


def _expand_matrices():
    e = np.zeros((2, DT_PAD, SSD_DIM), np.float32)
    for d in range(2):
        for h in range(SSD_HEADS):
            e[d, d * SSD_HEADS + h, h * SSD_HEAD_DIM:(h + 1) * SSD_HEAD_DIM] = 1.0
    return jnp.asarray(e, BF16)


def _ssd_small_params(l, ssd_dt_bias, ssd_A_log, ssd_D):
    n_dt = 2 * SSD_HEADS
    alog = ssd_A_log[l]
    alog_small = jnp.zeros((2, 1, DT_PAD), F32)
    for d in range(2):
        alog_small = alog_small.at[d, 0, d * SSD_HEADS:(d + 1) * SSD_HEADS].set(alog[d])
    return dict(
        dt_bias=jnp.pad(ssd_dt_bias[l].reshape(1, n_dt), ((0, 0), (0, DT_PAD - n_dt))),
        alog_small=alog_small,
        expand=_expand_matrices(),
        d_skip=jnp.repeat(ssd_D[l], SSD_HEAD_DIM)[None],
    )


def kernel(x_prompt, x_sample, c, cache_attn_k, cache_attn_v, state_ssd, c_ctx, w_mod, b_mod, norm1_g, w_in, pool_w, pool_scale, att_lambda, att_subln_g, ssd_conv_w, ssd_conv_b, ssd_dt_bias, ssd_A_log, ssd_D, ssd_norm_g, w_gate, b_gate, w_branch, w_out, norm2_g, ffn_w_up, ffn_conv_w, ffn_conv_b, ffn_w_down, final_norm_g):
    x = jnp.concatenate([x_prompt.reshape(N_CTX, D_MODEL), x_sample.reshape(N_LAT, D_MODEL)], axis=0)
    cvec = jnp.concatenate([c_ctx[None], c, jnp.zeros((MOD_ROWS - 1 - DEC_BATCH, D_MODEL), F32)], axis=0)
    mod = _modulation(cvec, w_mod, b_mod).reshape(DEPTH, MOD_ROWS, N_MOD, D_MODEL)
    rope_cos, rope_sin = _rope_tables()
    past_k = cache_attn_k.reshape(DEC_BATCH, DEPTH, PAST_LEN, ATT_QK_TOTAL).astype(BF16)
    past_v = cache_attn_v.reshape(DEC_BATCH, DEPTH, PAST_LEN, ATT_DIM).astype(BF16)
    past_state = state_ssd.reshape(DEC_BATCH, DEPTH, 2, SSD_DIM, SSD_STATE)

    n_dt = 2 * SSD_HEADS
    w_in_b = w_in.astype(BF16)
    w_dt_b = jnp.pad(w_in[:, :, _C_DT:], ((0, 0), (0, 0), (0, DT_PAD - n_dt))).astype(BF16)
    w_gate_b, w_branch_b, w_out_b = w_gate.astype(BF16), w_branch.astype(BF16), w_out.astype(BF16)
    w_up_b, w_down_b, pool_w_b = ffn_w_up.astype(BF16), ffn_w_down.astype(BF16), pool_w.astype(BF16)
    row = lambda a: a[:, None, :]
    norm1_r, norm2_r, b_gate_r, pool_scale_r = row(norm1_g), row(norm2_g), row(b_gate), row(pool_scale)
    subln_r, conv_b_r, ssd_norm_r, ffn_conv_b_r = row(att_subln_g), row(ssd_conv_b), row(ssd_norm_g), row(ffn_conv_b)
    final_g = final_norm_g[None]

    cache_shape = (BATCH, DEPTH, SEQ, ATT_QK_TOTAL)
    caches = (jnp.zeros(cache_shape, F32), jnp.zeros(cache_shape, F32))
    new_state = jnp.zeros((BATCH, DEPTH, 2, SSD_DIM, SSD_STATE), F32)
    for l in range(DEPTH):
        pooled, q, k_cache, v_cache, kb, vb, z, xbc, dt_raw = _inproj(x, mod[l], norm1_r, w_in_b, w_dt_b,
                                                                      rope_cos, rope_sin, l, caches)
        caches = (k_cache, v_cache)
        o_att = _diff_attention(q, kb, vb, past_k, past_v, att_lambda, subln_r, l)
        o_ssd, new_state = _ssd_mixer(xbc, dt_raw, z, past_state, l, ssd_conv_w, conv_b_r, ssd_norm_r,
                                      _ssd_small_params(l, ssd_dt_bias, ssd_A_log, ssd_D), new_state)
        x = _merge(x, mod[l], norm1_r, w_gate_b, b_gate_r, pooled, pool_w_b, pool_scale_r, o_att, o_ssd,
                   w_branch_b, w_out_b, l)
        x = _conv_ffn(x, mod[l], norm2_r, w_up_b, ffn_conv_w, ffn_conv_b_r, w_down_b, final_g, l,
                      final=(l == DEPTH - 1))

    y_ctx, y_lat = x
    return (y_ctx.reshape(BATCH, SEQ, D_MODEL), y_lat.reshape(DEC_BATCH, DEC_SEQ, D_MODEL),
            caches[0].reshape(BATCH, DEPTH, SEQ, ATT_HEADS, 2, ATT_QK_DIM),
            caches[1].reshape(BATCH, DEPTH, SEQ, ATT_HEADS, ATT_V_DIM),
            new_state.reshape(BATCH, DEPTH, 2, SSD_HEADS, SSD_HEAD_DIM, SSD_STATE))
```

```python
import functools
import math

import numpy as np
import jax
import jax.numpy as jnp
from jax import lax
from jax.experimental import pallas as pl
from jax.experimental.pallas import tpu as pltpu

D_MODEL = 1024
BATCH = 32
SEQ = 256
DEPTH = 4
DEC_BATCH = 4
DEC_SEQ = 2048
PAST_LEN = 256
GRID_W = 64
EPS = 1e-6
N_MOD = 6

POOL_GROUPS = 4
POOL_WINDOWS = (2, 4, 8, 16)
POOL_DIM = D_MODEL // 2
POOL_GROUP_DIM = POOL_DIM // POOL_GROUPS

ATT_HEADS = 4
ATT_QK_DIM = 64
ATT_V_DIM = 2 * ATT_QK_DIM
ATT_QK_TOTAL = ATT_HEADS * 2 * ATT_QK_DIM
ATT_DIM = ATT_HEADS * ATT_V_DIM
ROPE_BASE = 10000.0

SSD_DIM = D_MODEL // 2
SSD_HEAD_DIM = 64
SSD_HEADS = SSD_DIM // SSD_HEAD_DIM
SSD_GROUPS = 2
SSD_STATE = 128
SSD_CHUNK = 128
SSD_XBC_DIM = SSD_DIM + 2 * SSD_GROUPS * SSD_STATE
SSD_GROUP_DIM = SSD_DIM // SSD_GROUPS

MIX_DIM = POOL_DIM + ATT_DIM + SSD_DIM
FFN_DIM = ((8 * D_MODEL // 3 + 127) // 128) * 128

LANE = 128
SUBLANE = 8
HALO = SUBLANE

TILE = 256
N_CTX = BATCH * SEQ
N_LAT = DEC_BATCH * DEC_SEQ
N_TOK = N_CTX + N_LAT
NT_CTX = N_CTX // TILE
NT = N_TOK // TILE
TILES_PER_LAT = DEC_SEQ // TILE
N_HALO_BLOCKS = N_TOK // HALO
MOD_ROWS = 8
DT_PAD = LANE
MXU_WIDTH = 256
_FFN_SPLIT = (FFN_DIM // MXU_WIDTH + 1) // 2 * MXU_WIDTH
FFN_CHUNKS = ((0, _FFN_SPLIT), (_FFN_SPLIT, FFN_DIM))
VMEM_LIMIT = 56 * 1024 * 1024

ROW_TILE = 512
NRT = N_TOK // ROW_TILE
NRT_CTX = N_CTX // ROW_TILE
ROW_TILES_PER_LAT = DEC_SEQ // ROW_TILE
ROW_GROUPS = ROW_TILE // SUBLANE

KEY_BLOCK = 256
N_KEYS_LAT = PAST_LEN + DEC_SEQ

F32 = jnp.float32
BF16 = jnp.bfloat16

assert SEQ == TILE and DEC_SEQ % TILE == 0 and PAST_LEN == TILE
assert TILE == 2 * SSD_CHUNK and SSD_STATE == LANE and FFN_DIM % MXU_WIDTH == 0
assert ROW_TILE % SEQ == 0 and DEC_SEQ % ROW_TILE == 0
assert KEY_BLOCK == PAST_LEN == SEQ and DEC_SEQ % KEY_BLOCK == 0 and KEY_BLOCK == 2 * LANE


def _dot(a, b):
    return jnp.dot(a, b, preferred_element_type=F32)


def _dot_nt(a, b):
    return lax.dot_general(a, b, (((1,), (1,)), ((), ())), preferred_element_type=F32)


def _split3(x):
    x1 = x.astype(BF16)
    r1 = x - x1.astype(F32)
    x2 = r1.astype(BF16)
    x3 = (r1 - x2.astype(F32)).astype(BF16)
    return x1, x2, x3


def _dot_split(a, b_bf16, terms):
    acc = None
    for piece in _split3(a)[:terms]:
        part = _dot(piece, b_bf16)
        acc = part if acc is None else acc + part
    return acc


def _dot_exact_lhs(a_bf16, b):
    b1, b2, b3 = _split3(b)
    return _dot(a_bf16, b1) + _dot(a_bf16, b2) + _dot(a_bf16, b3)


def _silu(x):
    return x * jax.nn.sigmoid(x)


def _lat_batch(t):
    return jnp.maximum(t - NT_CTX, 0) // TILES_PER_LAT


def _const_spec(shape):
    nd = len(shape)
    return pl.BlockSpec(shape, lambda *_: (0,) * nd, pipeline_mode=pl.Buffered(1))


def _layer_spec(shape, lidx):
    nd = len(shape)
    return pl.BlockSpec((1,) + tuple(shape), lambda *_: (lidx,) + (0,) * nd,
                        pipeline_mode=pl.Buffered(1))


def _row_spec(cols):
    return pl.BlockSpec((TILE, cols), lambda i: (i, 0))


def _same_order(i):
    return i


def _lat_first_rows(i):
    return (i + NRT_CTX) % NRT


def _big_row_spec(cols, order=_same_order):
    return pl.BlockSpec((ROW_TILE, cols), lambda i: (order(i), 0))


def _big_halo_spec(cols, which, order=_same_order):
    per_tile = ROW_TILE // HALO
    if which == "prev":
        return pl.BlockSpec((HALO, cols), lambda i: (jnp.maximum(order(i) * per_tile - 1, 0), 0))
    return pl.BlockSpec((HALO, cols),
                        lambda i: (jnp.minimum((order(i) + 1) * per_tile, N_HALO_BLOCKS - 1), 0))


def _big_lat_index(t):
    rel = jnp.maximum(t - NRT_CTX, 0)
    return t < NRT_CTX, rel // ROW_TILES_PER_LAT, rel % ROW_TILES_PER_LAT


def _big_mod_spec(order=_same_order):
    def idx(i):
        is_ctx, b, _ = _big_lat_index(order(i))
        return (jnp.where(is_ctx, 0, 1 + b), 0, 0)
    return pl.BlockSpec((1, N_MOD, D_MODEL), idx)


def _params(n_axes=1):
    return pltpu.CompilerParams(dimension_semantics=("arbitrary",) * n_axes,
                                vmem_limit_bytes=VMEM_LIMIT)


def _rmsnorm_rows(x, g):
    ms = jnp.mean(x * x, axis=-1, keepdims=True)
    return x * lax.rsqrt(ms + EPS) * g


MOD_COLS = 1536


def _mod_kernel(c_ref, w_ref, b_ref, o_ref):
    s = _silu(c_ref[...]).astype(BF16)
    o_ref[0] = _dot(s, w_ref[0].astype(BF16)) + b_ref[0]


def _modulation(cvec, w_mod, b_mod):
    nmod = N_MOD * D_MODEL
    return pl.pallas_call(
        _mod_kernel,
        out_shape=jax.ShapeDtypeStruct((DEPTH, MOD_ROWS, nmod), F32),
        grid=(DEPTH, nmod // MOD_COLS),
        in_specs=[pl.BlockSpec((MOD_ROWS, D_MODEL), lambda l, j: (0, 0)),
                  pl.BlockSpec((1, D_MODEL, MOD_COLS), lambda l, j: (l, 0, j)),
                  pl.BlockSpec((1, 1, MOD_COLS), lambda l, j: (l, 0, j))],
        out_specs=pl.BlockSpec((1, MOD_ROWS, MOD_COLS), lambda l, j: (l, 0, j)),
        compiler_params=_params(2),
        name="modulation",
    )(cvec, w_mod, b_mod.reshape(DEPTH, 1, nmod))


_C_POOL = 0
_C_Q = _C_POOL + POOL_DIM
_C_K = _C_Q + ATT_QK_TOTAL
_C_V = _C_K + ATT_QK_TOTAL
_C_Z = _C_V + ATT_DIM
_C_XBC = _C_Z + SSD_DIM
_C_DT = _C_XBC + SSD_XBC_DIM
IN_DIM = _C_DT + 2 * SSD_HEADS
ROPE_SWAP = ATT_QK_DIM // 4
Q_PRESCALE = ATT_QK_DIM ** -0.5 * math.log2(math.e)


def _rope_block(xb, cos, sin):
    lane = lax.broadcasted_iota(jnp.int32, xb.shape, 1)
    first_half = (lane % (2 * ROPE_SWAP)) < ROPE_SWAP
    partner = jnp.where(first_half,
                        pltpu.roll(xb, LANE - ROPE_SWAP, axis=1),
                        pltpu.roll(xb, ROPE_SWAP, axis=1))
    return xb * cos + partner * sin


def _modulated_norm(x, g, mod_ref, shift_row):
    return (_rmsnorm_rows(x, g) * (1.0 + mod_ref[0, shift_row + 1:shift_row + 2, :])
            + mod_ref[0, shift_row:shift_row + 1, :])


def _x_tile_specs(x, order):
    if not isinstance(x, tuple):
        return [_big_row_spec(D_MODEL, order)], [x]
    ctx_idx = lambda i: (jnp.where(order(i) < NRT_CTX, order(i), 0), 0)
    lat_idx = lambda i: (jnp.maximum(order(i) - NRT_CTX, 0), 0)
    return ([pl.BlockSpec((ROW_TILE, D_MODEL), ctx_idx), pl.BlockSpec((ROW_TILE, D_MODEL), lat_idx)],
            list(x))


def _x_halo_specs(x, order):
    if not isinstance(x, tuple):
        return ([_big_halo_spec(D_MODEL, "prev", order), _big_halo_spec(D_MODEL, "next", order)], [x, x])
    per_tile = ROW_TILE // HALO
    last = N_LAT // HALO - 1
    prev_idx = lambda i: (jnp.clip((order(i) - NRT_CTX) * per_tile - 1, 0, last), 0)
    next_idx = lambda i: (jnp.clip((order(i) - NRT_CTX + 1) * per_tile, 0, last), 0)
    return ([pl.BlockSpec((HALO, D_MODEL), prev_idx), pl.BlockSpec((HALO, D_MODEL), next_idx)],
            [x[1], x[1]])


def _load_x_tile(x_refs, is_ctx):
    if len(x_refs) == 1:
        return x_refs[0][...]
    return jnp.where(is_ctx, x_refs[0][...], x_refs[1][...])


def _inproj_kernel(*refs, n_x):
    x_refs = refs[:n_x]
    (xp_ref, xn_ref, mod_ref, g_ref, w_ref, wdt_ref, cos_ref, sin_ref, kc_in_ref, vc_in_ref,
     pooled_ref, q_ref, kc_ref, vc_ref, kb_ref, vb_ref, z_ref, xbc_ref, dt_ref, pool_scr) = refs[n_x:]
    del kc_in_ref, vc_in_ref
    is_ctx, _, j = _big_lat_index(_lat_first_rows(pl.program_id(0)))
    seqs = ROW_TILE // SEQ
    g = g_ref[0]
    hb = _modulated_norm(_load_x_tile(x_refs, is_ctx), g, mod_ref, 0).astype(BF16)
    w = w_ref.at[0]
    h_halo = jnp.concatenate([_modulated_norm(xp_ref[...], g, mod_ref, 0),
                              _modulated_norm(xn_ref[...], g, mod_ref, 0)], axis=0).astype(BF16)
    u = _dot(jnp.concatenate([hb, h_halo], axis=0), w[:, _C_POOL:_C_Q])
    q = _dot(hb, w[:, _C_Q:_C_K])
    k = _dot(hb, w[:, _C_K:_C_V])
    pooled_ref[...] = _pooled_deviation(u[0:ROW_TILE], u[ROW_TILE:ROW_TILE + HALO],
                                        u[ROW_TILE + HALO:], pool_scr, is_ctx, j)
    cos = cos_ref[...]
    sin = sin_ref[...]
    for blk in range(ATT_QK_TOTAL // LANE):
        sl = slice(blk * LANE, (blk + 1) * LANE)
        q_ref[:, sl] = (_rope_block(q[:, sl], cos, sin) * Q_PRESCALE).astype(BF16)
        kr = _rope_block(k[:, sl], cos, sin)
        kc_ref[:, 0, :, sl] = kr.reshape(seqs, SEQ, LANE)
        kb_ref[:, sl] = kr.astype(BF16)
    v = _dot(hb, w[:, _C_V:_C_Z])
    vc_ref[:, 0] = v.reshape(seqs, SEQ, ATT_DIM)
    vb_ref[...] = v.astype(BF16)
    z_ref[...] = _dot(hb, w[:, _C_Z:_C_XBC])
    dt_ref[...] = _dot(hb, wdt_ref[0])
    xbc_ref[...] = _dot(hb, w[:, _C_XBC:_C_DT])


def _inproj(x, mod_l, norm_g, w_in, w_dt, rope_cos, rope_sin, lidx, caches):
    order = _lat_first_rows
    seqs = ROW_TILE // SEQ
    cache_shape = (BATCH, DEPTH, SEQ, ATT_QK_TOTAL)
    row_outs = {0: (POOL_DIM, BF16), 1: (ATT_QK_TOTAL, BF16), 4: (ATT_QK_TOTAL, BF16), 5: (ATT_DIM, BF16),
                6: (SSD_DIM, F32), 7: (SSD_XBC_DIM, F32), 8: (DT_PAD, F32)}

    def cache_idx(i):
        t = order(i)
        return (jnp.where(t < NRT_CTX, t, 0), lidx, 0, 0)

    def rope_idx(i):
        is_ctx, _, j = _big_lat_index(order(i))
        return (jnp.where(is_ctx, 0, 1 + j), 0)

    out_shape, out_specs = [], []
    for n in range(9):
        if n in row_outs:
            c, d = row_outs[n]
            out_shape.append(jax.ShapeDtypeStruct((N_TOK, c), d))
            out_specs.append(_big_row_spec(c, order))
        else:
            out_shape.append(jax.ShapeDtypeStruct(cache_shape, F32))
            out_specs.append(pl.BlockSpec((seqs, 1, SEQ, ATT_QK_TOTAL), cache_idx))
    x_specs, x_args = _x_tile_specs(x, order)
    halo_specs, halo_args = _x_halo_specs(x, order)
    in_specs = (x_specs + halo_specs
                + [_big_mod_spec(order),
                   _layer_spec((1, D_MODEL), lidx),
                   _layer_spec((D_MODEL, IN_DIM), lidx),
                   _layer_spec((D_MODEL, DT_PAD), lidx),
                   pl.BlockSpec((ROW_TILE, LANE), rope_idx),
                   pl.BlockSpec((ROW_TILE, LANE), rope_idx)]
                + [pl.BlockSpec(memory_space=pl.ANY)] * 2)
    args = x_args + halo_args + [mod_l, norm_g, w_in, w_dt, rope_cos, rope_sin, *caches]
    return pl.pallas_call(
        functools.partial(_inproj_kernel, n_x=len(x_args)),
        out_shape=out_shape,
        grid=(NRT,),
        in_specs=in_specs,
        out_specs=out_specs,
        scratch_shapes=[pltpu.VMEM((POOL_SCR_ROWS, POOL_DIM), F32)],
        input_output_aliases={len(args) - 2: 2, len(args) - 1: 3},
        compiler_params=_params(),
        name="inproj",
    )(*args)


POOL_HALF_STRIDE = SEQ + 2 * HALO
POOL_SCR_ROWS = (ROW_TILE // SEQ) * POOL_HALF_STRIDE


def _fill_halves(scr, cur, prev, nxt, is_ctx, j):
    n_half = ROW_TILE // SEQ
    first = jnp.logical_or(is_ctx, j == 0)
    last = jnp.logical_or(is_ctx, j == ROW_TILES_PER_LAT - 1)
    for hf in range(n_half):
        base = hf * POOL_HALF_STRIDE
        if hf == 0:
            before = jnp.where(first, 0.0, prev)
        else:
            before = jnp.where(is_ctx, 0.0, cur[hf * SEQ - HALO:hf * SEQ, :])
        if hf == n_half - 1:
            after = jnp.where(last, 0.0, nxt)
        else:
            after = jnp.where(is_ctx, 0.0, cur[(hf + 1) * SEQ:(hf + 1) * SEQ + HALO, :])
        scr[base:base + HALO, :] = before
        scr[base + HALO:base + HALO + SEQ, :] = cur[hf * SEQ:(hf + 1) * SEQ, :]
        scr[base + HALO + SEQ:base + POOL_HALF_STRIDE, :] = after


def _pooled_deviation(u, u_prev, u_next, scr, is_ctx, j):
    n_half = ROW_TILE // SEQ
    seq_len = jnp.where(is_ctx, SEQ, DEC_SEQ)
    _fill_halves(scr, u, u_prev, u_next, is_ctx, j)
    halves = []
    for hf in range(n_half):
        base = hf * POOL_HALF_STRIDE + HALO
        pos = (jnp.where(is_ctx, 0, j * ROW_TILE + hf * SEQ)
               + lax.broadcasted_iota(jnp.int32, (SEQ, 1), 0))
        groups = []
        for g, win in enumerate(POOL_WINDOWS):
            half = win // 2
            sl = slice(g * POOL_GROUP_DIM, (g + 1) * POOL_GROUP_DIM)
            acc = scr[base - half:base - half + SEQ, sl]
            for k in range(1 - half, half):
                acc = acc + scr[base + k:base + k + SEQ, sl]
            lo = jnp.maximum(pos - half, 0)
            hi = jnp.minimum(pos + half - 1, seq_len - 1)
            inv_count = 1.0 / (hi - lo + 1).astype(F32)
            groups.append((acc * inv_count - scr[base:base + SEQ, sl]).astype(BF16))
        halves.append(jnp.concatenate(groups, axis=1))
    return jnp.concatenate(halves, axis=0)


def _attn_kernel(q_ref, kc_ref, vc_ref, kl_ref, vl_ref, kp_ref, vp_ref, lam_ref, g_ref, o_ref,
                 s_scr, *, lam_init):
    i = pl.program_id(0)
    lp = lam_ref[0]
    lam = (jnp.exp(jnp.sum(lp[0:1] * lp[1:2], axis=-1, keepdims=True))
           - jnp.exp(jnp.sum(lp[2:3] * lp[3:4], axis=-1, keepdims=True)) + lam_init)

    def stacked_queries(h):
        qh = q_ref[:, h * LANE:(h + 1) * LANE]
        lane = lax.broadcasted_iota(jnp.int32, qh.shape, 1)
        zero = jnp.zeros_like(qh)
        return jnp.concatenate([jnp.where(lane < ATT_QK_DIM, qh, zero),
                                jnp.where(lane >= ATT_QK_DIM, qh, zero)], axis=0)

    def run(key_blocks):
        def score_block(q2, h, b, buf, run_max):
            kr, _, r0 = key_blocks[b]
            s = _dot_nt(q2, kr[r0:r0 + KEY_BLOCK, h * LANE:(h + 1) * LANE])
            s_scr[buf, :, b * KEY_BLOCK:(b + 1) * KEY_BLOCK] = s
            blk_max = jnp.maximum(s[:, 0:LANE], s[:, LANE:KEY_BLOCK])
            return blk_max if run_max is None else jnp.maximum(run_max, blk_max)

        run_max = None
        q2 = stacked_queries(0)
        for b in range(len(key_blocks)):
            run_max = score_block(q2, 0, b, 0, run_max)
        for h in range(ATT_HEADS):
            sl = slice(h * LANE, (h + 1) * LANE)
            buf = h % 2
            m = jnp.max(run_max, axis=-1, keepdims=True)
            run_max, run_sum, acc = None, None, None
            if h + 1 < ATT_HEADS:
                q2 = stacked_queries(h + 1)
            for b, (_, vr, r0) in enumerate(key_blocks):
                if h + 1 < ATT_HEADS:
                    run_max = score_block(q2, h + 1, b, 1 - buf, run_max)
                p = jnp.exp2(s_scr[buf, :, b * KEY_BLOCK:(b + 1) * KEY_BLOCK] - m)
                blk_sum = p[:, 0:LANE] + p[:, LANE:KEY_BLOCK]
                run_sum = blk_sum if run_sum is None else run_sum + blk_sum
                part = _dot(p.astype(BF16), vr[r0:r0 + KEY_BLOCK, sl])
                acc = part if acc is None else acc + part
            o2 = acc / jnp.sum(run_sum, axis=-1, keepdims=True)
            oh = o2[0:TILE] - lam * o2[TILE:]
            oh = _rmsnorm_rows(oh, g_ref[0]) * (1.0 - lam_init)
            o_ref[:, sl] = oh.astype(BF16)

    @pl.when(i < NT_CTX)
    def _():
        run([(kc_ref, vc_ref, 0)])

    @pl.when(i >= NT_CTX)
    def _():
        run([(kp_ref.at[0, 0], vp_ref.at[0, 0], 0)]
            + [(kl_ref, vl_ref, r0) for r0 in range(0, DEC_SEQ, KEY_BLOCK)])


def _diff_attention(q, kb, vb, past_k, past_v, att_lambda, subln_g, lidx):
    lam_init = 0.8 - 0.6 * math.exp(-0.3 * lidx)
    ctx_idx = lambda i: (jnp.minimum(i, NT_CTX - 1), 0)
    lat_idx = lambda i: (N_CTX // DEC_SEQ + _lat_batch(i), 0)
    past_idx = lambda i: (_lat_batch(i), lidx, 0, 0)
    return pl.pallas_call(
        functools.partial(_attn_kernel, lam_init=lam_init),
        out_shape=jax.ShapeDtypeStruct((N_TOK, ATT_DIM), BF16),
        grid=(NT,),
        in_specs=[_row_spec(ATT_QK_TOTAL),
                  pl.BlockSpec((TILE, ATT_QK_TOTAL), ctx_idx),
                  pl.BlockSpec((TILE, ATT_DIM), ctx_idx),
                  pl.BlockSpec((DEC_SEQ, ATT_QK_TOTAL), lat_idx),
                  pl.BlockSpec((DEC_SEQ, ATT_DIM), lat_idx),
                  pl.BlockSpec((1, 1, PAST_LEN, ATT_QK_TOTAL), past_idx),
                  pl.BlockSpec((1, 1, PAST_LEN, ATT_DIM), past_idx),
                  _layer_spec((4, ATT_QK_DIM), lidx),
                  _layer_spec((1, ATT_V_DIM), lidx)],
        out_specs=_row_spec(ATT_DIM),
        scratch_shapes=[pltpu.VMEM((2, 2 * TILE, N_KEYS_LAT), F32)],
        compiler_params=_params(),
        name="diff_attention",
    )(q, kb, vb, kb, vb, past_k, past_v, att_lambda, subln_g)


def _ssd_scan_tile(direction, t, xc_scr, dt_ref, h0_ref, dtb_ref, asmall_ref, e_ref, st_ref,
                   state_scr, y_scr):
    is_ctx, _, j = _big_lat_index(t)
    seq_begins = (j == 0) if direction == 0 else (j == ROW_TILES_PER_LAT - 1)

    @pl.when(is_ctx)
    def _():
        state_scr[...] = jnp.zeros_like(state_scr)

    @pl.when(jnp.logical_and(jnp.logical_not(is_ctx), seq_begins))
    def _():
        state_scr[...] = h0_ref[0, 0, 0].T

    a_small = -jnp.exp(asmall_ref[...])
    row = lax.broadcasted_iota(jnp.int32, (SSD_CHUNK, SSD_CHUNK), 0)
    col = lax.broadcasted_iota(jnp.int32, (SSD_CHUNK, SSD_CHUNK), 1)
    keep = (col <= row) if direction == 0 else (col >= row)
    tri = jnp.where(keep, 1.0, 0.0).astype(BF16)
    pair_lane = lax.broadcasted_iota(jnp.int32, (SSD_CHUNK, LANE), 1)
    edge = SSD_CHUNK - 1 if direction == 0 else 0
    heads_per_group = SSD_HEADS // SSD_GROUPS

    chunks = range(ROW_TILE // SSD_CHUNK)
    rows = [slice(c * SSD_CHUNK, (c + 1) * SSD_CHUNK) for c in chunks]
    groups = [slice(g * SSD_GROUP_DIM, (g + 1) * SSD_GROUP_DIM) for g in range(SSD_GROUPS)]
    dts = [jax.nn.softplus(dt_ref[r, :] + dtb_ref[...]) for r in rows]
    dt_exp = [_dot_split(dt, e_ref[...], 2) for dt in dts]
    acs_s = [_dot_exact_lhs(tri, dt * a_small) for dt in dts]
    acs_exp = [_dot_split(a, e_ref[...], 3) for a in acs_s]
    acs_st = [a.T for a in acs_s]
    b_mat = [[xc_scr[r, SSD_DIM + g * SSD_STATE:SSD_DIM + (g + 1) * SSD_STATE]
              for g in range(SSD_GROUPS)] for r in rows]
    c_mat = [[xc_scr[r, SSD_DIM + (SSD_GROUPS + g) * SSD_STATE:
                     SSD_DIM + (SSD_GROUPS + g + 1) * SSD_STATE].astype(BF16)
              for g in range(SSD_GROUPS)] for r in rows]
    cb = [[_dot_nt(c_mat[c][g], b_mat[c][g].astype(BF16)) for g in range(SSD_GROUPS)] for c in chunks]
    xdt = [xc_scr[rows[c], 0:SSD_DIM] * dt_exp[c] for c in chunks]
    total = [a[edge:edge + 1, :] for a in acs_exp]
    xw = [(xdt[c] * jnp.exp(total[c] - acs_exp[c])).astype(BF16) for c in chunks]
    decay_from_entry = [jnp.exp(a) for a in acs_exp]
    y_diag = []
    for c in chunks:
        per_group = []
        for g in range(SSD_GROUPS):
            pairs = []
            for pr in range(heads_per_group // 2):
                ps = slice(g * SSD_GROUP_DIM + pr * LANE, g * SSD_GROUP_DIM + (pr + 1) * LANE)
                x_pair = xdt[c][:, ps].astype(BF16)
                halves = []
                for hh in range(2):
                    hcol = direction * SSD_HEADS + g * heads_per_group + pr * 2 + hh
                    diff = acs_s[c][:, hcol:hcol + 1] - acs_st[c][hcol:hcol + 1, :]
                    decay = jnp.where(keep, jnp.exp(diff), 0.0)
                    halves.append(_dot((cb[c][g] * decay).astype(BF16), x_pair))
                pairs.append(jnp.where(pair_lane < SSD_HEAD_DIM, halves[0], halves[1]))
            per_group.append(jnp.concatenate(pairs, axis=1))
        y_diag.append(per_group)
    state_in = [_dot(b_mat[c][g].T.astype(BF16), xw[c][:, groups[g]])
                for c in chunks for g in range(SSD_GROUPS)]

    h = [state_scr[:, gs] for gs in groups]
    chunks_per_seq = SEQ // SSD_CHUNK
    scan_order = list(chunks) if direction == 0 else list(reversed(chunks))
    for n, c in enumerate(scan_order):
        if n > 0 and n % chunks_per_seq == 0:
            h = [jnp.where(is_ctx, 0.0, hg) for hg in h]
        for g, gs in enumerate(groups):
            y_off = _dot(c_mat[c][g], h[g].astype(BF16)) * decay_from_entry[c][:, gs]
            y_scr[rows[c], gs] = y_diag[c][g] + y_off
            h[g] = h[g] * jnp.exp(total[c][:, gs]) + state_in[c * SSD_GROUPS + g]
        if (n + 1) % chunks_per_seq == 0:
            st_ref[c // chunks_per_seq, 0, 0] = jnp.concatenate(h, axis=1).T
    for g, gs in enumerate(groups):
        state_scr[:, gs] = h[g]


def _ssd_fwd_kernel(xbc_ref, xp_ref, xn_ref, dt_ref, h0_ref, cw_ref, cb_ref, dtb_ref, asmall_ref, e_ref,
                    st_in_ref, y_ref, xc_ref, st_ref, conv_scr, state_scr):
    del st_in_ref
    t = _lat_first_rows(pl.program_id(0))
    is_ctx, _, j = _big_lat_index(t)
    _fill_halves(conv_scr, xbc_ref, xp_ref[...], xn_ref[...], is_ctx, j)
    for hf in range(ROW_TILE // SEQ):
        b = hf * POOL_HALF_STRIDE + HALO
        conv = (cw_ref[0, 0:1, :] * conv_scr[b - 1:b - 1 + SEQ, :]
                + cw_ref[0, 1:2, :] * conv_scr[b:b + SEQ, :]
                + cw_ref[0, 2:3, :] * conv_scr[b + 1:b + 1 + SEQ, :] + cb_ref[0])
        xc_ref[hf * SEQ:(hf + 1) * SEQ, :] = _silu(conv)
    _ssd_scan_tile(0, t, xc_ref, dt_ref, h0_ref, dtb_ref, asmall_ref, e_ref, st_ref, state_scr, y_ref)


def _ssd_bwd_kernel(xc_ref, dt_ref, h0_ref, dtb_ref, asmall_ref, e_ref, yf_ref, z_ref, dskip_ref, ng_ref,
                    st_in_ref, o_ref, st_ref, state_scr, y_scr):
    del st_in_ref
    t = NRT - 1 - pl.program_id(0)
    _ssd_scan_tile(1, t, xc_ref, dt_ref, h0_ref, dtb_ref, asmall_ref, e_ref, st_ref, state_scr, y_scr)
    y = yf_ref[...] + y_scr[...] + dskip_ref[...] * xc_ref[:, 0:SSD_DIM]
    yz = y * _silu(z_ref[...])
    for g in range(SSD_GROUPS):
        gs = slice(g * SSD_GROUP_DIM, (g + 1) * SSD_GROUP_DIM)
        o_ref[:, gs] = _rmsnorm_rows(yz[:, gs], ng_ref[0, :, gs]).astype(BF16)


def _ssd_mixer(xbc, dt_raw, z, state_ssd, lidx, conv_w, conv_b, norm_g, p, new_state):
    st_shape = jax.ShapeDtypeStruct((BATCH, DEPTH, 2, SSD_DIM, SSD_STATE), F32)
    st_block = (ROW_TILE // SEQ, 1, 1, SSD_DIM, SSD_STATE)
    rev = lambda i: NRT - 1 - i
    fwd = _lat_first_rows
    state_scratch = pltpu.VMEM((SSD_STATE, SSD_DIM), F32)
    any_spec = pl.BlockSpec(memory_space=pl.ANY)

    def st_idx(order, direction, parked):
        def idx(i):
            t = order(i)
            return (jnp.where(t < NRT_CTX, t, parked), lidx, direction, 0, 0)
        return idx

    def scan_specs(order, direction):
        h0_idx = lambda i: (_big_lat_index(order(i))[1], lidx, direction, 0, 0)
        return [_big_row_spec(DT_PAD, order),
                pl.BlockSpec((1, 1, 1, SSD_DIM, SSD_STATE), h0_idx),
                _const_spec((1, DT_PAD)), _const_spec((1, DT_PAD)), _const_spec((DT_PAD, SSD_DIM))]

    def scan_args(direction):
        return [dt_raw, state_ssd, p["dt_bias"], p["alog_small"][direction], p["expand"][direction]]

    fwd_args = [xbc, xbc, xbc, *scan_args(0)[:2], conv_w, conv_b, *scan_args(0)[2:], new_state]
    y_f, xc, new_state = pl.pallas_call(
        _ssd_fwd_kernel,
        out_shape=[jax.ShapeDtypeStruct((N_TOK, SSD_DIM), F32),
                   jax.ShapeDtypeStruct((N_TOK, SSD_XBC_DIM), F32), st_shape],
        grid=(NRT,),
        in_specs=([_big_row_spec(SSD_XBC_DIM, fwd), _big_halo_spec(SSD_XBC_DIM, "prev", fwd),
                   _big_halo_spec(SSD_XBC_DIM, "next", fwd)]
                  + scan_specs(fwd, 0)[:2]
                  + [_layer_spec((3, SSD_XBC_DIM), lidx), _layer_spec((1, SSD_XBC_DIM), lidx)]
                  + scan_specs(fwd, 0)[2:] + [any_spec]),
        out_specs=[_big_row_spec(SSD_DIM, fwd), _big_row_spec(SSD_XBC_DIM, fwd),
                   pl.BlockSpec(st_block, st_idx(fwd, 0, 0))],
        scratch_shapes=[pltpu.VMEM((POOL_SCR_ROWS, SSD_XBC_DIM), F32), state_scratch],
        input_output_aliases={len(fwd_args) - 1: 2},
        compiler_params=_params(),
        name="ssd_forward",
    )(*fwd_args)

    bwd_args = [xc, *scan_args(1), y_f, z, p["d_skip"], norm_g, new_state]
    out, new_state = pl.pallas_call(
        _ssd_bwd_kernel,
        out_shape=[jax.ShapeDtypeStruct((N_TOK, SSD_DIM), BF16), st_shape],
        grid=(NRT,),
        in_specs=([_big_row_spec(SSD_XBC_DIM, rev)] + scan_specs(rev, 1)
                  + [_big_row_spec(SSD_DIM, rev), _big_row_spec(SSD_DIM, rev),
                     _const_spec((1, SSD_DIM)), _layer_spec((1, SSD_DIM), lidx), any_spec]),
        out_specs=[_big_row_spec(SSD_DIM, rev), pl.BlockSpec(st_block, st_idx(rev, 1, NRT_CTX - 1))],
        scratch_shapes=[state_scratch, pltpu.VMEM((ROW_TILE, SSD_DIM), F32)],
        input_output_aliases={len(bwd_args) - 1: 1},
        compiler_params=_params(),
        name="ssd_backward",
    )(*bwd_args)
    return out, new_state


def _merge_kernel(*refs, n_x):
    x_refs = refs[:n_x]
    (mod_ref, g_ref, wg_ref, bg_ref, pooled_ref, pw_ref, ps_ref, oa_ref, os_ref, wb_ref, wo_ref,
     o_ref) = refs[n_x:]
    x = _load_x_tile(x_refs, pl.program_id(0) < NRT_CTX)
    hb = _modulated_norm(x, g_ref[0], mod_ref, 0).astype(BF16)
    wg = wg_ref.at[0]
    wb = wb_ref.at[0]

    def gated(n, branch, row0, width):
        cols = slice(n * D_MODEL, (n + 1) * D_MODEL)
        gate = jax.nn.sigmoid(_dot(hb, wg[:, cols]) + bg_ref[0, :, cols])
        return gate * _dot(branch, wb[row0:row0 + width, :])

    o_pool = jnp.concatenate(
        [_dot(pooled_ref[:, g * POOL_GROUP_DIM:(g + 1) * POOL_GROUP_DIM], pw_ref[0, g])
         for g in range(POOL_GROUPS)], axis=1) * ps_ref[0]
    merged = (gated(0, o_pool.astype(BF16), 0, POOL_DIM)
              + gated(1, oa_ref[...], POOL_DIM, ATT_DIM)
              + gated(2, os_ref[...], POOL_DIM + ATT_DIM, SSD_DIM))
    o_ref[...] = x + mod_ref[0, 2:3, :] * _dot(merged.astype(BF16), wo_ref[0])


def _merge(x, mod_l, norm_g, w_gate, b_gate, pooled, pool_w, pool_scale, o_att, o_ssd, w_branch, w_out,
           lidx):
    x_specs, x_args = _x_tile_specs(x, _same_order)
    return pl.pallas_call(
        functools.partial(_merge_kernel, n_x=len(x_args)),
        out_shape=jax.ShapeDtypeStruct((N_TOK, D_MODEL), F32),
        grid=(NRT,),
        in_specs=x_specs + [_big_mod_spec(),
                  _layer_spec((1, D_MODEL), lidx),
                  _layer_spec((D_MODEL, 3 * D_MODEL), lidx), _layer_spec((1, 3 * D_MODEL), lidx),
                  _big_row_spec(POOL_DIM),
                  _layer_spec((POOL_GROUPS, POOL_GROUP_DIM, POOL_GROUP_DIM), lidx),
                  _layer_spec((1, POOL_DIM), lidx),
                  _big_row_spec(ATT_DIM), _big_row_spec(SSD_DIM),
                  _layer_spec((MIX_DIM, D_MODEL), lidx), _layer_spec((D_MODEL, D_MODEL), lidx)],
        out_specs=_big_row_spec(D_MODEL),
        compiler_params=_params(),
        name="merge",
    )(*x_args, mod_l, norm_g, w_gate, b_gate, pooled, pool_w, pool_scale, o_att, o_ssd, w_branch, w_out)


N_FFN_IN = 10


def _ffn_kernel(*refs, final):
    x_ref, xp_ref, xn_ref, mod_ref, g_ref, wu_ref, cw_ref, cb_ref, wd_ref, fg_ref = refs[:N_FFN_IN]
    h_scr, cols_scr = refs[-2:]
    out_refs = refs[N_FFN_IN:-2]
    res_ref = h_scr if final else out_refs[0]
    i = pl.program_id(0)
    is_ctx, _, j = _big_lat_index(i)
    first = jnp.logical_or(is_ctx, j == 0)
    last = jnp.logical_or(is_ctx, j == ROW_TILES_PER_LAT - 1)
    seq_len = jnp.where(is_ctx, SEQ, DEC_SEQ)
    tile_pos = jnp.where(is_ctx, 0, j * ROW_TILE)
    n_col = D_MODEL // LANE

    g = g_ref[0]
    h_nat = _modulated_norm(x_ref[...], g, mod_ref, 3)
    for cb in range(n_col):
        cols_scr[cb] = h_nat[:, cb * LANE:(cb + 1) * LANE]
    for r in range(ROW_GROUPS):
        for cb in range(n_col):
            h_scr[r * SUBLANE:(r + 1) * SUBLANE, cb * LANE:(cb + 1) * LANE] = (
                cols_scr[cb, pl.ds(r, SUBLANE, stride=ROW_GROUPS), :])
    h_scr[ROW_TILE:ROW_TILE + HALO, :] = jnp.where(first, 0.0, _modulated_norm(xp_ref[...], g, mod_ref, 3))
    h_scr[ROW_TILE + HALO:, :] = jnp.where(last, 0.0, _modulated_norm(xn_ref[...], g, mod_ref, 3))
    hb = h_scr[...].astype(BF16)

    sub = lax.broadcasted_iota(jnp.int32, (SUBLANE, 1), 0)
    pos_first = tile_pos + sub * ROW_GROUPS
    pos_last = pos_first + (ROW_GROUPS - 1)
    at_seq_start = (pos_first % seq_len) == 0
    at_seq_end = (pos_last % seq_len) == seq_len - 1
    wu = wu_ref.at[0]

    def up_proj(c):
        lo, hi = FFN_CHUNKS[c]
        return (_dot(hb, wu[:, lo:hi]), _dot(hb, wu[:, FFN_DIM + lo:FFN_DIM + hi]))

    def conv_half(up, col0):
        cols = slice(col0, col0 + up.shape[1])
        a = up[0:ROW_TILE]
        halo_prev = up[ROW_TILE + HALO - 1:ROW_TILE + HALO]
        halo_next = up[ROW_TILE + HALO:ROW_TILE + HALO + 1]
        wrap_prev = pltpu.roll(a[ROW_TILE - SUBLANE:ROW_TILE], 1, axis=0)
        wrap_prev = jnp.where(sub == 0, halo_prev, wrap_prev)
        wrap_prev = jnp.where(at_seq_start, 0.0, wrap_prev)
        wrap_next = pltpu.roll(a[0:SUBLANE], SUBLANE - 1, axis=0)
        wrap_next = jnp.where(sub == SUBLANE - 1, halo_next, wrap_next)
        wrap_next = jnp.where(at_seq_end, 0.0, wrap_next)
        before = jnp.concatenate([wrap_prev, a[0:ROW_TILE - SUBLANE]], axis=0)
        after = jnp.concatenate([a[SUBLANE:ROW_TILE], wrap_next], axis=0)
        return (cw_ref[0, 0:1, cols] * before + cw_ref[0, 1:2, cols] * a
                + cw_ref[0, 2:3, cols] * after + cb_ref[0, :, cols])

    acc = None
    n_chunks = len(FFN_CHUNKS)
    up_next = up_proj(0)
    for c, (lo, hi) in enumerate(FFN_CHUNKS):
        up_gate, up_val = up_next
        if c + 1 < n_chunks:
            up_next = up_proj(c + 1)
        a_gate = conv_half(up_gate, lo)
        a_val = conv_half(up_val, FFN_DIM + lo)
        act = (_silu(a_gate) * a_val).astype(BF16)
        part = _dot(act, wd_ref[0, lo:hi, :])
        acc = part if acc is None else acc + part
    upd = mod_ref[0, 5:6, :] * acc
    for cb in range(n_col):
        cols_scr[cb] = upd[:, cb * LANE:(cb + 1) * LANE]
    groups_per_sub = ROW_GROUPS // SUBLANE
    for q in range(ROW_GROUPS):
        start = (q % groups_per_sub) * ROW_GROUPS + q // groups_per_sub
        rows = slice(q * SUBLANE, (q + 1) * SUBLANE)
        for cb in range(n_col):
            cs = slice(cb * LANE, (cb + 1) * LANE)
            res_ref[rows, cs] = x_ref[rows, cs] + cols_scr[cb, pl.ds(start, SUBLANE, stride=SUBLANE), :]
    if final:
        y = _rmsnorm_rows(res_ref[0:ROW_TILE, :], fg_ref[...])
        y_ctx_ref, y_lat_ref = out_refs

        @pl.when(is_ctx)
        def _():
            y_ctx_ref[...] = y

        @pl.when(jnp.logical_not(is_ctx))
        def _():
            y_lat_ref[...] = y


def _conv_ffn(x, mod_l, norm_g, w_up, conv_w, conv_b, w_down, final_g, lidx, final):
    if final:
        out_shape = [jax.ShapeDtypeStruct((N_CTX, D_MODEL), F32), jax.ShapeDtypeStruct((N_LAT, D_MODEL), F32)]
        out_specs = [pl.BlockSpec((ROW_TILE, D_MODEL), lambda i: (jnp.minimum(i, NRT_CTX - 1), 0)),
                     pl.BlockSpec((ROW_TILE, D_MODEL), lambda i: (jnp.maximum(i - NRT_CTX, 0), 0))]
    else:
        out_shape = jax.ShapeDtypeStruct((N_TOK, D_MODEL), F32)
        out_specs = _big_row_spec(D_MODEL)
    return pl.pallas_call(
        functools.partial(_ffn_kernel, final=final),
        out_shape=out_shape,
        grid=(NRT,),
        in_specs=[_big_row_spec(D_MODEL), _big_halo_spec(D_MODEL, "prev"), _big_halo_spec(D_MODEL, "next"),
                  _big_mod_spec(),
                  _layer_spec((1, D_MODEL), lidx),
                  _layer_spec((D_MODEL, 2 * FFN_DIM), lidx), _layer_spec((3, 2 * FFN_DIM), lidx),
                  _layer_spec((1, 2 * FFN_DIM), lidx), _layer_spec((FFN_DIM, D_MODEL), lidx),
                  _const_spec((1, D_MODEL))],
        out_specs=out_specs,
        scratch_shapes=[pltpu.VMEM((ROW_TILE + 2 * HALO, D_MODEL), F32),
                        pltpu.VMEM((D_MODEL // LANE, ROW_TILE, LANE), F32)],
        compiler_params=_params(),
        name="conv_ffn",
    )(x, x, x, mod_l, norm_g, w_up, conv_w, conv_b, w_down, final_g)


def _rope_tables():
    rows = DEC_SEQ // GRID_W
    row = jnp.repeat(jnp.arange(rows), GRID_W).astype(F32)
    col = jnp.tile(jnp.arange(GRID_W), rows).astype(F32)
    half = ATT_QK_DIM // 2
    inv = ROPE_BASE ** (-jnp.arange(0, half, 2, dtype=F32) / half)
    ar, ac = row[:, None] * inv, col[:, None] * inv
    cos = jnp.concatenate([jnp.cos(ar), jnp.cos(ar), jnp.cos(ac), jnp.cos(ac)], axis=-1)
    sin = jnp.concatenate([-jnp.sin(ar), jnp.sin(ar), -jnp.sin(ac), jnp.sin(ac)], axis=-1)
    reps = LANE // ATT_QK_DIM
    cos = jnp.concatenate([jnp.ones((ROW_TILE, LANE), F32), jnp.tile(cos, (1, reps))], axis=0)
    sin = jnp.concatenate([jnp.zeros((ROW_TILE, LANE), F32), jnp.tile(sin, (1, reps))], axis=0)
    return cos, sin


def _expand_matrices():
    e = np.zeros((2, DT_PAD, SSD_DIM), np.float32)
    for d in range(2):
        for h in range(SSD_HEADS):
            e[d, d * SSD_HEADS + h, h * SSD_HEAD_DIM:(h + 1) * SSD_HEAD_DIM] = 1.0
    return jnp.asarray(e, BF16)


def _ssd_small_params(l, ssd_dt_bias, ssd_A_log, ssd_D):
    n_dt = 2 * SSD_HEADS
    alog = ssd_A_log[l]
    alog_small = jnp.zeros((2, 1, DT_PAD), F32)
    for d in range(2):
        alog_small = alog_small.at[d, 0, d * SSD_HEADS:(d + 1) * SSD_HEADS].set(alog[d])
    return dict(
        dt_bias=jnp.pad(ssd_dt_bias[l].reshape(1, n_dt), ((0, 0), (0, DT_PAD - n_dt))),
        alog_small=alog_small,
        expand=_expand_matrices(),
        d_skip=jnp.repeat(ssd_D[l], SSD_HEAD_DIM)[None],
    )


def kernel(x_prompt, x_sample, c, cache_attn_k, cache_attn_v, state_ssd, c_ctx, w_mod, b_mod, norm1_g, w_in, pool_w, pool_scale, att_lambda, att_subln_g, ssd_conv_w, ssd_conv_b, ssd_dt_bias, ssd_A_log, ssd_D, ssd_norm_g, w_gate, b_gate, w_branch, w_out, norm2_g, ffn_w_up, ffn_conv_w, ffn_conv_b, ffn_w_down, final_norm_g):
    x = (x_prompt.reshape(N_CTX, D_MODEL), x_sample.reshape(N_LAT, D_MODEL))
    cvec = jnp.concatenate([c_ctx[None], c, jnp.zeros((MOD_ROWS - 1 - DEC_BATCH, D_MODEL), F32)], axis=0)
    mod = _modulation(cvec, w_mod, b_mod).reshape(DEPTH, MOD_ROWS, N_MOD, D_MODEL)
    rope_cos, rope_sin = _rope_tables()
    past_k = cache_attn_k.reshape(DEC_BATCH, DEPTH, PAST_LEN, ATT_QK_TOTAL).astype(BF16)
    past_v = cache_attn_v.reshape(DEC_BATCH, DEPTH, PAST_LEN, ATT_DIM).astype(BF16)
    past_state = state_ssd.reshape(DEC_BATCH, DEPTH, 2, SSD_DIM, SSD_STATE)

    n_dt = 2 * SSD_HEADS
    w_in_b = w_in.astype(BF16)
    w_dt_b = jnp.pad(w_in[:, :, _C_DT:], ((0, 0), (0, 0), (0, DT_PAD - n_dt))).astype(BF16)
    w_gate_b, w_branch_b, w_out_b = w_gate.astype(BF16), w_branch.astype(BF16), w_out.astype(BF16)
    w_up_b, w_down_b, pool_w_b = ffn_w_up.astype(BF16), ffn_w_down.astype(BF16), pool_w.astype(BF16)
    row = lambda a: a[:, None, :]
    norm1_r, norm2_r, b_gate_r, pool_scale_r = row(norm1_g), row(norm2_g), row(b_gate), row(pool_scale)
    subln_r, conv_b_r, ssd_norm_r, ffn_conv_b_r = row(att_subln_g), row(ssd_conv_b), row(ssd_norm_g), row(ffn_conv_b)
    final_g = final_norm_g[None]

    cache_shape = (BATCH, DEPTH, SEQ, ATT_QK_TOTAL)
    caches = (jnp.zeros(cache_shape, F32), jnp.zeros(cache_shape, F32))
    new_state = jnp.zeros((BATCH, DEPTH, 2, SSD_DIM, SSD_STATE), F32)
    for l in range(DEPTH):
        pooled, q, k_cache, v_cache, kb, vb, z, xbc, dt_raw = _inproj(x, mod[l], norm1_r, w_in_b, w_dt_b,
                                                                      rope_cos, rope_sin, l, caches)
        caches = (k_cache, v_cache)
        o_att = _diff_attention(q, kb, vb, past_k, past_v, att_lambda, subln_r, l)
        o_ssd, new_state = _ssd_mixer(xbc, dt_raw, z, past_state, l, ssd_conv_w, conv_b_r, ssd_norm_r,
                                      _ssd_small_params(l, ssd_dt_bias, ssd_A_log, ssd_D), new_state)
        x = _merge(x, mod[l], norm1_r, w_gate_b, b_gate_r, pooled, pool_w_b, pool_scale_r, o_att, o_ssd,
                   w_branch_b, w_out_b, l)
        x = _conv_ffn(x, mod[l], norm2_r, w_up_b, ffn_conv_w, ffn_conv_b_r, w_down_b, final_g, l,
                      final=(l == DEPTH - 1))

    y_ctx, y_lat = x
    return (y_ctx.reshape(BATCH, SEQ, D_MODEL), y_lat.reshape(DEC_BATCH, DEC_SEQ, D_MODEL),
            caches[0].reshape(BATCH, DEPTH, SEQ, ATT_HEADS, 2, ATT_QK_DIM),
            caches[1].reshape(BATCH, DEPTH, SEQ, ATT_HEADS, ATT_V_DIM),
            new_state.reshape(BATCH, DEPTH, 2, SSD_HEADS, SSD_HEAD_DIM, SSD_STATE))
```
